```python
import jax, jax.numpy as jnp
from jax import lax
import numpy as np

D_MODEL = 1024
BATCH = 8
SEQ = 4096
DEPTH = 2

D_MIX = D_MODEL
MLA_HEADS = 6
MLA_NOPE = 64
MLA_ROPE = 32
MLA_V = 64
Q_RANK = 256
KV_RANK = 128
MLA_WIDTH = MLA_HEADS * MLA_V
FOX_HEADS = 6
FOX_DIM = 64
FOX_WIDTH = FOX_HEADS * FOX_DIM
CONV_WIDTH = D_MIX - MLA_WIDTH - FOX_WIDTH
CONV_K = 3
D_FF = 2816
PLE_DIM = 256
BLOCK = 128
ROPE_THETA = 10000.0
EPS = 1e-6

IN_SIZES = (Q_RANK, KV_RANK, MLA_ROPE,
            CONV_WIDTH, CONV_WIDTH, CONV_WIDTH,
            FOX_WIDTH, FOX_WIDTH, FOX_WIDTH, FOX_HEADS)
N_IN = sum(IN_SIZES)
IN_SPLITS = [sum(IN_SIZES[:j + 1]) for j in range(len(IN_SIZES) - 1)]

kernel_name = "hymba_style_mla_shortconv_fox_hybrid"


def rmsnorm(x, g):
    xf = x.astype(jnp.float32)
    y = xf * lax.rsqrt(jnp.mean(xf * xf, axis=-1, keepdims=True) + EPS)
    return (y * g.astype(jnp.float32)).astype(x.dtype)


def rope_tables(positions):
    inv_freq = ROPE_THETA ** (-jnp.arange(0, MLA_ROPE, 2, dtype=jnp.float32) / MLA_ROPE)
    ang = positions.astype(jnp.float32)[..., None] * inv_freq
    return jnp.cos(ang), jnp.sin(ang)


def apply_rope(x, cos, sin):
    xf = x.astype(jnp.float32)
    x1, x2 = jnp.split(xf, 2, axis=-1)
    out = jnp.concatenate([x1 * cos - x2 * sin, x1 * sin + x2 * cos], axis=-1)
    return out.astype(x.dtype)


def causal_dwconv(x, w, b=None):
    s = x.shape[1]
    xp = jnp.pad(x, ((0, 0), (CONV_K - 1, 0), (0, 0)))
    y = xp[:, 0:s] * w[0]
    for j in range(1, CONV_K):
        y = y + xp[:, j:j + s] * w[j]
    if b is not None:
        y = y + b
    return y


def blocked_causal_attention(q, k, v, log_decay_cum=None):
    b, s, h, dk = q.shape
    nb = s // BLOCK
    scale = dk ** -0.5
    q_blocks = q.reshape(b, nb, BLOCK, h, dk).swapaxes(0, 1)
    starts = jnp.arange(nb, dtype=jnp.int32) * BLOCK
    k_pos = jnp.arange(s, dtype=jnp.int32)
    if log_decay_cum is None:
        xs = (q_blocks, starts)
    else:
        c_blocks = log_decay_cum.reshape(b, h, nb, BLOCK).transpose(2, 0, 1, 3)
        xs = (q_blocks, starts, c_blocks)

    def attend(blk):
        qi, start = blk[0], blk[1]
        sc = jnp.einsum('bqhd,bkhd->bhqk', qi, k).astype(jnp.float32) * scale
        if log_decay_cum is not None:
            sc = sc + blk[2][..., :, None] - log_decay_cum[:, :, None, :]
        q_pos = start + jnp.arange(BLOCK, dtype=jnp.int32)
        mask = k_pos[None, :] <= q_pos[:, None]
        sc = jnp.where(mask, sc, -jnp.inf)
        pr = jax.nn.softmax(sc, axis=-1).astype(v.dtype)
        return jnp.einsum('bhqk,bkhd->bqhd', pr, v)

    o = lax.map(attend, xs)
    return o.swapaxes(0, 1).reshape(b, s, h, v.shape[-1])


def mla_mixer(zq, zkv, zr, cos, sin, q_norm, w_uq, kv_norm, w_ukv):
    b, s, _ = zq.shape
    q = (rmsnorm(zq, q_norm) @ w_uq).reshape(b, s, MLA_HEADS, MLA_NOPE + MLA_ROPE)
    q_nope, q_rope = q[..., :MLA_NOPE], q[..., MLA_NOPE:]
    q_rope = apply_rope(q_rope, cos[:, :, None, :], sin[:, :, None, :])
    kv = (rmsnorm(zkv, kv_norm) @ w_ukv).reshape(b, s, MLA_HEADS, MLA_NOPE + MLA_V)
    k_nope, v = kv[..., :MLA_NOPE], kv[..., MLA_NOPE:]
    k_rope = apply_rope(zr, cos, sin)
    k = jnp.concatenate([k_nope, jnp.broadcast_to(k_rope[:, :, None, :], (b, s, MLA_HEADS, MLA_ROPE))], axis=-1)
    q = jnp.concatenate([q_nope, q_rope], axis=-1)
    return blocked_causal_attention(q, k, v).reshape(b, s, MLA_WIDTH)


def conv_mixer(zb, zc, zh, conv_w):
    return zb * causal_dwconv(zc * zh, conv_w)


def fox_mixer(fq, fk, fv, ff, b_forget):
    b, s, _ = fq.shape
    q = fq.reshape(b, s, FOX_HEADS, FOX_DIM)
    k = fk.reshape(b, s, FOX_HEADS, FOX_DIM)
    v = fv.reshape(b, s, FOX_HEADS, FOX_DIM)
    log_f = jax.nn.log_sigmoid(ff.astype(jnp.float32) + b_forget.astype(jnp.float32))
    cum = jnp.cumsum(log_f, axis=1).transpose(0, 2, 1)
    return blocked_causal_attention(q, k, v, cum).reshape(b, s, FOX_WIDTH)


def conv_glu_ffn(m, w_up, conv_w, conv_b, w_down):
    u = causal_dwconv(m @ w_up, conv_w, conv_b)
    g, val = jnp.split(u, 2, axis=-1)
    return (jax.nn.silu(g) * val) @ w_down


def setup_inputs(seed: int = 0) -> dict:
    key = jax.random.key(seed)
    ks = jax.random.split(key, 24)
    f32 = jnp.float32

    def nrm(k, shape, fan_in):
        return jax.random.normal(k, shape, f32) * (fan_in ** -0.5)

    def gain(k, shape):
        return 1.0 + 0.05 * jax.random.normal(k, shape, f32)

    x = jax.random.normal(ks[0], (BATCH, SEQ, D_MODEL), f32)
    p = jax.random.normal(ks[1], (DEPTH, BATCH, SEQ, PLE_DIM), f32)
    offsets = jax.random.randint(ks[2], (BATCH, 1), 0, 1024, dtype=jnp.int32)
    positions = offsets + jnp.arange(SEQ, dtype=jnp.int32)[None, :]
    return {
        "x": x,
        "p": p,
        "positions": positions,
        "attn_norm": gain(ks[3], (DEPTH, D_MODEL)),
        "w_in": nrm(ks[4], (DEPTH, D_MODEL, N_IN), D_MODEL),
        "b_forget": 2.0 + 0.1 * jax.random.normal(ks[5], (DEPTH, FOX_HEADS), f32),
        "q_norm": gain(ks[6], (DEPTH, Q_RANK)),
        "w_uq": nrm(ks[7], (DEPTH, Q_RANK, MLA_HEADS * (MLA_NOPE + MLA_ROPE)), Q_RANK),
        "kv_norm": gain(ks[8], (DEPTH, KV_RANK)),
        "w_ukv": nrm(ks[9], (DEPTH, KV_RANK, MLA_HEADS * (MLA_NOPE + MLA_V)), KV_RANK),
        "conv_w": nrm(ks[10], (DEPTH, CONV_K, CONV_WIDTH), CONV_K),
        "mla_out_norm": gain(ks[11], (DEPTH, MLA_WIDTH)),
        "conv_out_norm": gain(ks[12], (DEPTH, CONV_WIDTH)),
        "fox_out_norm": gain(ks[13], (DEPTH, FOX_WIDTH)),
        "w_out": nrm(ks[14], (DEPTH, D_MIX, D_MODEL), D_MIX),
        "ffn_norm": gain(ks[15], (DEPTH, D_MODEL)),
        "w_up": nrm(ks[16], (DEPTH, D_MODEL, 2 * D_FF), D_MODEL),
        "ffn_conv_w": nrm(ks[17], (DEPTH, CONV_K, 2 * D_FF), CONV_K),
        "ffn_conv_b": 0.02 * jax.random.normal(ks[18], (DEPTH, 2 * D_FF), f32),
        "w_down": nrm(ks[19], (DEPTH, D_FF, D_MODEL), D_FF),
        "ple_norm": gain(ks[20], (DEPTH, D_MODEL)),
        "w_ple_gate": nrm(ks[21], (DEPTH, D_MODEL, D_MODEL), D_MODEL),
        "w_ple": nrm(ks[22], (DEPTH, PLE_DIM, D_MODEL), PLE_DIM),
        "final_norm": gain(ks[23], (D_MODEL,)),
    }


def reference(x, p, positions, attn_norm, w_in, b_forget, q_norm, w_uq, kv_norm, w_ukv,
              conv_w, mla_out_norm, conv_out_norm, fox_out_norm, w_out, ffn_norm, w_up,
              ffn_conv_w, ffn_conv_b, w_down, ple_norm, w_ple_gate, w_ple, final_norm):
    cos, sin = rope_tables(positions)
    h = x
    for i in range(DEPTH):
        a = rmsnorm(h, attn_norm[i])
        z = a @ w_in[i]
        zq, zkv, zr, zb, zc, zh, fq, fk, fv, ff = jnp.split(z, IN_SPLITS, axis=-1)
        o_mla = mla_mixer(zq, zkv, zr, cos, sin, q_norm[i], w_uq[i], kv_norm[i], w_ukv[i])
        o_conv = conv_mixer(zb, zc, zh, conv_w[i])
        o_fox = fox_mixer(fq, fk, fv, ff, b_forget[i])
        mixed = jnp.concatenate([rmsnorm(o_mla, mla_out_norm[i]),
                                 rmsnorm(o_conv, conv_out_norm[i]),
                                 rmsnorm(o_fox, fox_out_norm[i])], axis=-1)
        h = h + mixed @ w_out[i]
        m = rmsnorm(h, ffn_norm[i])
        h = h + conv_glu_ffn(m, w_up[i], ffn_conv_w[i], ffn_conv_b[i], w_down[i])
        gate = jax.nn.sigmoid(rmsnorm(h, ple_norm[i]) @ w_ple_gate[i])
        h = h + gate * (p[i] @ w_ple[i])
    return rmsnorm(h, final_norm)
```

```python
import functools
import math

import jax
import jax.numpy as jnp
from jax import lax
from jax.experimental import pallas as pl
from jax.experimental.pallas import tpu as pltpu

F32 = jnp.float32
BF16 = jnp.bfloat16

D_MODEL = 1024
MLA_HEADS = 6
MLA_NOPE = 64
MLA_ROPE = 32
MLA_V = 64
Q_RANK = 256
KV_RANK = 128
FOX_HEADS = 6
FOX_DIM = 64
CONV_WIDTH = 256
CONV_K = 3
D_FF = 2816
PLE_DIM = 256
ROPE_THETA = 10000.0
EPS = 1e-6

LANES = 128
BF16_SUBLANES = 16
V7X_VMEM_BYTES = 64 * 1024 * 1024
VMEM_LIMIT = V7X_VMEM_BYTES - 8 * 1024 * 1024

HEADS = MLA_HEADS + FOX_HEADS
HW = MLA_HEADS * LANES
ATT_W = HEADS * 64

C_ZQ = 0
C_ZKV = C_ZQ + Q_RANK
C_ZR = C_ZKV + KV_RANK
C_ZRS = C_ZR + LANES
C_ZB = C_ZRS + LANES
C_ZC = C_ZB + CONV_WIDTH
C_ZH = C_ZC + CONV_WIDTH
C_FQ = C_ZH + CONV_WIDTH
C_FK = C_FQ + HW
C_FV = C_FK + HW
C_FF = C_FV + HW
N_Z = C_FF + LANES

L_QC = 64
L_KC = 67

TS_IN = 512
TQ = 512
TS_OUT = 512
TS_FFN = 512
TF_FFN = 1408
HALO = BF16_SUBLANES
TS_PLE = 512


def _rms(x, g):
    return x * lax.rsqrt(jnp.mean(x * x, axis=-1, keepdims=True) + EPS) * g


def _split3(x):
    hi = x.astype(BF16)
    r = x - hi.astype(F32)
    mid = r.astype(BF16)
    lo = (r - mid.astype(F32)).astype(BF16)
    return hi, mid, lo


def _dot(a, b):
    return jnp.dot(a, b, preferred_element_type=F32)


def _in_kernel(x_ref, g_ref, win_ref, qn_ref, wuq_ref, kvn_ref, wukv_ref, cos_ref, sin_ref,
               cw_ref, cn_ref, bf_ref, pq_ref, pk_ref, rows_ref,
               q_ref, k_ref, v_ref, oc_ref, xs_sc, cum_sc, *, ts):
    @pl.when(pl.program_id(1) == 0)
    def _():
        xs_sc[0:8, :] = jnp.zeros((8, CONV_WIDTH), F32)
        cum_sc[...] = jnp.zeros_like(cum_sc)

    a = _rms(x_ref[0], g_ref[...]).astype(BF16)

    def proj(lo, width):
        return _dot(a, win_ref[:, lo:lo + width])

    cos_t = cos_ref[0]
    sin_t = sin_ref[0]

    qn = _rms(proj(C_ZQ, Q_RANK), qn_ref[...]).astype(BF16)
    q_main = _dot(qn, wuq_ref[:, 0:HW])
    q_swap = _dot(qn, wuq_ref[:, HW:2 * HW])
    mla_scale = (MLA_NOPE + MLA_ROPE) ** -0.5
    kvn = _rms(proj(C_ZKV, KV_RANK), kvn_ref[...]).astype(BF16)
    k_nope = _dot(kvn, wukv_ref[:, 0:HW])
    v_mla = _dot(kvn, wukv_ref[:, HW:2 * HW])
    k_rope = proj(C_ZR, LANES) * cos_t + proj(C_ZRS, LANES) * sin_t
    for h in range(MLA_HEADS):
        sl = slice(h * LANES, (h + 1) * LANES)
        q_ref[0, h] = ((q_main[:, sl] * cos_t + q_swap[:, sl] * sin_t) * mla_scale).astype(BF16)
        k_ref[0, h] = (k_nope[:, sl] + k_rope).astype(BF16)
        v_ref[0, h] = (v_mla[:, sl] + rows_ref[0:1, sl]).astype(BF16)

    xc = proj(C_ZC, CONV_WIDTH) * proj(C_ZH, CONV_WIDTH)
    xs_sc[8:8 + ts, :] = xc
    y = (cw_ref[0:1, :] * xs_sc[6:6 + ts, :] + cw_ref[1:2, :] * xs_sc[7:7 + ts, :]
         + cw_ref[2:3, :] * xc)
    oc_ref[0] = _rms(proj(C_ZB, CONV_WIDTH) * y, cn_ref[...]).astype(BF16)
    xs_sc[0:8, :] = xs_sc[ts:ts + 8, :]

    ff = proj(C_FF, LANES) + bf_ref[...]
    log_f = jnp.minimum(ff, 0.0) - jnp.log1p(jnp.exp(-jnp.abs(ff)))
    row = lax.broadcasted_iota(jnp.int32, (ts, ts), 0)
    col = lax.broadcasted_iota(jnp.int32, (ts, ts), 1)
    tri = (col <= row).astype(BF16)
    f_hi, f_mid, f_lo = _split3(log_f)
    cum = _dot(tri, f_hi) + _dot(tri, f_mid) + _dot(tri, f_lo) + cum_sc[0:1, :]
    cum_sc[0:1, :] = cum[ts - 1:ts, :]
    c_hi, c_mid, c_lo = _split3(cum)
    aug_q = _dot(c_hi, pq_ref[0]) + _dot(c_mid, pq_ref[1]) + _dot(c_lo, pq_ref[2])
    aug_k = _dot(c_hi, pk_ref[0]) + _dot(c_mid, pk_ref[1]) + _dot(c_lo, pk_ref[2])

    fq = proj(C_FQ, HW)
    fk = proj(C_FK, HW)
    fv = proj(C_FV, HW)
    fox_scale = FOX_DIM ** -0.5
    for h in range(FOX_HEADS):
        sl = slice(h * LANES, (h + 1) * LANES)
        q_ref[0, MLA_HEADS + h] = (fq[:, sl] * fox_scale + aug_q[:, sl] + rows_ref[1:2, sl]).astype(BF16)
        k_ref[0, MLA_HEADS + h] = (fk[:, sl] + aug_k[:, sl] + rows_ref[2:3, sl]).astype(BF16)
        v_ref[0, MLA_HEADS + h] = (fv[:, sl] + rows_ref[0:1, sl]).astype(BF16)


def _in_call(h, cos_t, sin_t, w, *, ts=TS_IN):
    b, s, _ = h.shape
    const = lambda shape: pl.BlockSpec(shape, lambda bi, si: (0,) * len(shape))
    head_spec = pl.BlockSpec((1, HEADS, ts, LANES), lambda bi, si: (bi, 0, si, 0))
    head_shape = jax.ShapeDtypeStruct((b, HEADS, s, LANES), BF16)
    return pl.pallas_call(
        functools.partial(_in_kernel, ts=ts),
        grid=(b, s // ts),
        in_specs=[
            pl.BlockSpec((1, ts, D_MODEL), lambda bi, si: (bi, si, 0)),
            const((1, D_MODEL)),
            const((D_MODEL, N_Z)),
            const((1, Q_RANK)),
            const((Q_RANK, 2 * HW)),
            const((1, KV_RANK)),
            const((KV_RANK, 2 * HW)),
            pl.BlockSpec((1, ts, LANES), lambda bi, si: (bi, si, 0)),
            pl.BlockSpec((1, ts, LANES), lambda bi, si: (bi, si, 0)),
            const((CONV_K, CONV_WIDTH)),
            const((1, CONV_WIDTH)),
            const((1, LANES)),
            const((3, LANES, HW)),
            const((3, LANES, HW)),
            const((3, HW)),
        ],
        out_specs=[head_spec, head_spec, head_spec,
                   pl.BlockSpec((1, ts, CONV_WIDTH), lambda bi, si: (bi, si, 0))],
        out_shape=[head_shape, head_shape, head_shape,
                   jax.ShapeDtypeStruct((b, s, CONV_WIDTH), BF16)],
        scratch_shapes=[pltpu.VMEM((ts + 8, CONV_WIDTH), F32), pltpu.VMEM((8, LANES), F32)],
        compiler_params=pltpu.CompilerParams(
            dimension_semantics=("arbitrary", "arbitrary"), vmem_limit_bytes=VMEM_LIMIT),
        name="in_proj",
    )(h, w["attn_norm"], w["w_in"], w["q_norm"], w["w_uq"], w["kv_norm"], w["w_ukv"],
      cos_t, sin_t, w["conv_w"], w["conv_out_norm"], w["b_forget"], w["place_q"], w["place_k"],
      w["rows"])


def _attn_kernel(q_ref, k_ref, v_ref, o_ref, m_sc, acc_sc, *, tq):
    i = pl.program_id(2)
    row = lax.broadcasted_iota(jnp.int32, (tq, tq), 0)
    col = lax.broadcasted_iota(jnp.int32, (tq, tq), 1)
    outs = []
    for hh in range(2):
        q = q_ref[0, hh]
        m_sc[...] = jnp.full_like(m_sc, -jnp.inf)
        acc_sc[...] = jnp.zeros_like(acc_sc)

        def step(j, masked, hh=hh, q=q):
            off = pl.multiple_of(j * tq, tq)
            k = k_ref[0, hh, pl.ds(off, tq), :]
            v = v_ref[0, hh, pl.ds(off, tq), :]
            s = lax.dot_general(q, k, (((1,), (1,)), ((), ())), preferred_element_type=F32)
            if masked:
                s = jnp.where(col <= row, s, -jnp.inf)
            m_prev = m_sc[...]
            m_new = jnp.maximum(m_prev, jnp.max(s, axis=-1, keepdims=True))
            alpha = jnp.exp(m_prev - m_new)
            p = jnp.exp(s - m_new).astype(BF16)
            acc_sc[...] = alpha * acc_sc[...] + _dot(p, v)
            m_sc[...] = m_new

        def body(j, carry):
            step(j, False)
            return carry

        lax.fori_loop(0, i, body, 0)
        step(i, True)
        acc = acc_sc[...]
        ones_lane = 64 if hh == 0 else 0
        outs.append(acc / acc[:, ones_lane:ones_lane + 1])
    lane = lax.broadcasted_iota(jnp.int32, (tq, LANES), 1)
    o_ref[0] = jnp.where(lane < 64, outs[0], outs[1])


def _attn_call(q, k, v, *, tq=TQ):
    b, _, s, _ = q.shape
    return pl.pallas_call(
        functools.partial(_attn_kernel, tq=tq),
        grid=(b, HEADS // 2, s // tq),
        in_specs=[
            pl.BlockSpec((1, 2, tq, LANES), lambda bi, g, i: (bi, g, i, 0)),
            pl.BlockSpec((1, 2, s, LANES), lambda bi, g, i: (bi, g, 0, 0)),
            pl.BlockSpec((1, 2, s, LANES), lambda bi, g, i: (bi, g, 0, 0)),
        ],
        out_specs=pl.BlockSpec((1, tq, LANES), lambda bi, g, i: (bi, i, g)),
        out_shape=jax.ShapeDtypeStruct((b, s, ATT_W), F32),
        scratch_shapes=[pltpu.VMEM((tq, 1), F32), pltpu.VMEM((tq, LANES), F32)],
        compiler_params=pltpu.CompilerParams(
            dimension_semantics=("arbitrary", "arbitrary", "arbitrary"),
            vmem_limit_bytes=VMEM_LIMIT),
        name="attention",
    )(q, k, v)


def _out_kernel(o_ref, oc_ref, h_ref, gm_ref, gf_ref, wout_ref, out_ref):
    o = o_ref[0]
    half = ATT_W // 2
    mixed = jnp.concatenate([
        _rms(o[:, :half], gm_ref[...]).astype(BF16),
        oc_ref[0],
        _rms(o[:, half:], gf_ref[...]).astype(BF16)], axis=-1)
    out_ref[0] = h_ref[0] + _dot(mixed, wout_ref[...])


def _out_call(o, oc, h, w, *, ts=TS_OUT):
    b, s, _ = h.shape
    const = lambda shape: pl.BlockSpec(shape, lambda bi, si: (0,) * len(shape))
    row = lambda width: pl.BlockSpec((1, ts, width), lambda bi, si: (bi, si, 0))
    return pl.pallas_call(
        _out_kernel,
        grid=(b, s // ts),
        in_specs=[row(ATT_W), row(CONV_WIDTH), row(D_MODEL),
                  const((1, ATT_W // 2)), const((1, ATT_W // 2)), const((D_MODEL, D_MODEL))],
        out_specs=row(D_MODEL),
        out_shape=jax.ShapeDtypeStruct(h.shape, F32),
        compiler_params=pltpu.CompilerParams(
            dimension_semantics=("arbitrary", "arbitrary"), vmem_limit_bytes=VMEM_LIMIT),
        name="out_proj",
    )(o, oc, h, w["mla_out_norm"], w["fox_out_norm"], w["w_out"])


def _ffn_kernel(hp_ref, h_ref, g_ref, wg_ref, wv_ref, cwg_ref, cwv_ref, cbg_ref, cbv_ref, wd_ref,
                out_ref, m_sc, *, ts):
    c = pl.program_id(2)

    @pl.when(c == 0)
    def _():
        keep = (pl.program_id(1) > 0).astype(F32)
        m_sc[0:HALO, :] = _rms(hp_ref[0] * keep, g_ref[...]).astype(BF16)
        m_sc[HALO:HALO + ts, :] = _rms(h_ref[0], g_ref[...]).astype(BF16)

    m = m_sc[...]

    def conv(u, cw_ref, cb_ref):
        return (cw_ref[0:1, :] * u[HALO - 2:HALO - 2 + ts, :]
                + cw_ref[1:2, :] * u[HALO - 1:HALO - 1 + ts, :]
                + cw_ref[2:3, :] * u[HALO:HALO + ts, :] + cb_ref[...])

    gate = conv(_dot(m, wg_ref[...]), cwg_ref, cbg_ref)
    val = conv(_dot(m, wv_ref[...]), cwv_ref, cbv_ref)
    act = (gate * jax.nn.sigmoid(gate) * val).astype(BF16)
    part = _dot(act, wd_ref[...])

    @pl.when(c == 0)
    def _():
        out_ref[0] = h_ref[0] + part

    @pl.when(c > 0)
    def _():
        out_ref[0] += part


def _ffn_call(h, w, *, ts=TS_FFN, tf=TF_FFN):
    b, s, _ = h.shape
    nf = D_FF // tf
    halo_blocks = ts // HALO
    return pl.pallas_call(
        functools.partial(_ffn_kernel, ts=ts),
        grid=(b, s // ts, nf),
        in_specs=[
            pl.BlockSpec((1, HALO, D_MODEL),
                         lambda bi, si, c: (bi, jnp.maximum(si * halo_blocks - 1, 0), 0)),
            pl.BlockSpec((1, ts, D_MODEL), lambda bi, si, c: (bi, si, 0)),
            pl.BlockSpec((1, D_MODEL), lambda bi, si, c: (0, 0)),
            pl.BlockSpec((D_MODEL, tf), lambda bi, si, c: (0, c)),
            pl.BlockSpec((D_MODEL, tf), lambda bi, si, c: (0, nf + c)),
            pl.BlockSpec((CONV_K, tf), lambda bi, si, c: (0, c)),
            pl.BlockSpec((CONV_K, tf), lambda bi, si, c: (0, nf + c)),
            pl.BlockSpec((1, tf), lambda bi, si, c: (0, c)),
            pl.BlockSpec((1, tf), lambda bi, si, c: (0, nf + c)),
            pl.BlockSpec((tf, D_MODEL), lambda bi, si, c: (c, 0)),
        ],
        out_specs=pl.BlockSpec((1, ts, D_MODEL), lambda bi, si, c: (bi, si, 0)),
        out_shape=jax.ShapeDtypeStruct(h.shape, F32),
        scratch_shapes=[pltpu.VMEM((ts + HALO, D_MODEL), BF16)],
        compiler_params=pltpu.CompilerParams(
            dimension_semantics=("arbitrary", "arbitrary", "arbitrary"),
            vmem_limit_bytes=VMEM_LIMIT),
        name="ffn",
    )(h, h, w["ffn_norm"], w["w_up"], w["w_up"], w["ffn_conv_w"], w["ffn_conv_w"],
      w["ffn_conv_b"], w["ffn_conv_b"], w["w_down"])


def _ple_kernel(h_ref, p_ref, g_ref, wpg_ref, wple_ref, fg_ref, out_ref, *, final):
    h = h_ref[0]
    gate = jax.nn.sigmoid(_dot(_rms(h, g_ref[...]).astype(BF16), wpg_ref[...]))
    out = h + gate * _dot(p_ref[0, 0].astype(BF16), wple_ref[...])
    if final:
        out = _rms(out, fg_ref[...])
    out_ref[0] = out


def _ple_call(h, p, layer, w, final_norm, *, final, ts=TS_PLE):
    b, s, _ = h.shape
    const = lambda shape: pl.BlockSpec(shape, lambda bi, si: (0,) * len(shape))
    row = pl.BlockSpec((1, ts, D_MODEL), lambda bi, si: (bi, si, 0))
    return pl.pallas_call(
        functools.partial(_ple_kernel, final=final),
        grid=(b, s // ts),
        in_specs=[row,
                  pl.BlockSpec((1, 1, ts, PLE_DIM), lambda bi, si: (layer, bi, si, 0)),
                  const((1, D_MODEL)), const((D_MODEL, D_MODEL)), const((PLE_DIM, D_MODEL)),
                  const((1, D_MODEL))],
        out_specs=row,
        out_shape=jax.ShapeDtypeStruct(h.shape, F32),
        compiler_params=pltpu.CompilerParams(
            dimension_semantics=("arbitrary", "arbitrary"), vmem_limit_bytes=VMEM_LIMIT),
        name="ple",
    )(h, p, w["ple_norm"], w["w_ple_gate"], w["w_ple"], final_norm)


def _head_rows(w, width, offsets):
    kdim = w.shape[0]
    w = w.reshape(kdim, MLA_HEADS, width)
    out = jnp.zeros((kdim, MLA_HEADS, LANES), w.dtype)
    for h, off in enumerate(offsets):
        out = out.at[:, h, off:off + width].set(w[:, h])
    return out.reshape(kdim, HW)


V_OFFSETS = tuple(0 if h % 2 == 0 else 64 for h in range(MLA_HEADS))


def _placement(lane0, sign):
    mats = []
    for t in range(3):
        m = jnp.zeros((LANES, MLA_HEADS, LANES), F32)
        for h in range(FOX_HEADS):
            m = m.at[h, h, lane0 + t].set(sign)
        mats.append(m.reshape(LANES, HW))
    return jnp.stack(mats).astype(BF16)


def _const_rows():
    rows = jnp.zeros((3, MLA_HEADS, LANES), F32)
    for h in range(MLA_HEADS):
        rows = rows.at[0, h, 64 if h % 2 == 0 else 0].set(1.0)
    rows = rows.at[1, :, L_KC:L_KC + 3].set(1.0)
    rows = rows.at[2, :, L_QC:L_QC + 3].set(1.0)
    return rows.reshape(3, HW)


def _layer_weights(i, attn_norm, w_in, b_forget, q_norm, w_uq, kv_norm, w_ukv, conv_w,
                   mla_out_norm, conv_out_norm, fox_out_norm, w_out, ffn_norm, w_up,
                   ffn_conv_w, ffn_conv_b, w_down, ple_norm, w_ple_gate, w_ple):
    sizes = (Q_RANK, KV_RANK, MLA_ROPE, CONV_WIDTH, CONV_WIDTH, CONV_WIDTH,
             FOX_HEADS * FOX_DIM, FOX_HEADS * FOX_DIM, FOX_HEADS * FOX_DIM, FOX_HEADS)
    splits = [sum(sizes[:j + 1]) for j in range(len(sizes) - 1)]
    wq, wkv, wr, wb, wc, wh, wfq, wfk, wfv, wff = jnp.split(w_in[i], splits, axis=-1)
    half = MLA_ROPE // 2
    zeros = lambda n: jnp.zeros((D_MODEL, n), F32)
    wr_swapped = jnp.concatenate([wr[:, half:], wr[:, :half]], axis=-1)
    w_in_p = jnp.concatenate([
        wq, wkv,
        zeros(MLA_NOPE), wr, zeros(LANES - MLA_NOPE - MLA_ROPE),
        zeros(MLA_NOPE), wr_swapped, zeros(LANES - MLA_NOPE - MLA_ROPE),
        wb, wc, wh,
        _head_rows(wfq, FOX_DIM, (0,) * FOX_HEADS),
        _head_rows(wfk, FOX_DIM, (0,) * FOX_HEADS),
        _head_rows(wfv, FOX_DIM, V_OFFSETS),
        wff, zeros(LANES - FOX_HEADS)], axis=-1).astype(BF16)

    uq = w_uq[i].reshape(Q_RANK, MLA_HEADS, MLA_NOPE + MLA_ROPE)
    uq_main = _head_rows(uq.reshape(Q_RANK, -1), MLA_NOPE + MLA_ROPE, (0,) * MLA_HEADS)
    uq_rot = jnp.concatenate([uq[..., MLA_NOPE + half:], uq[..., MLA_NOPE:MLA_NOPE + half]], axis=-1)
    uq_swap = _head_rows(uq_rot.reshape(Q_RANK, -1), MLA_ROPE, (MLA_NOPE,) * MLA_HEADS)
    ukv = w_ukv[i].reshape(KV_RANK, MLA_HEADS, MLA_NOPE + MLA_V)
    uk = _head_rows(ukv[..., :MLA_NOPE].reshape(KV_RANK, -1), MLA_NOPE, (0,) * MLA_HEADS)
    uv = _head_rows(ukv[..., MLA_NOPE:].reshape(KV_RANK, -1), MLA_V, V_OFFSETS)

    row = lambda v: v[i].reshape(1, -1)
    return {
        "attn_norm": row(attn_norm),
        "w_in": w_in_p,
        "q_norm": row(q_norm),
        "w_uq": jnp.concatenate([uq_main, uq_swap], axis=-1).astype(BF16),
        "kv_norm": row(kv_norm),
        "w_ukv": jnp.concatenate([uk, uv], axis=-1).astype(BF16),
        "conv_w": conv_w[i],
        "conv_out_norm": row(conv_out_norm),
        "b_forget": jnp.pad(b_forget[i], (0, LANES - FOX_HEADS)).reshape(1, LANES),
        "place_q": _placement(L_QC, 1.0),
        "place_k": _placement(L_KC, -1.0),
        "rows": _const_rows(),
        "mla_out_norm": row(mla_out_norm),
        "fox_out_norm": row(fox_out_norm),
        "w_out": w_out[i].astype(BF16),
        "ffn_norm": row(ffn_norm),
        "w_up": w_up[i].astype(BF16),
        "ffn_conv_w": ffn_conv_w[i],
        "ffn_conv_b": row(ffn_conv_b),
        "w_down": w_down[i].astype(BF16),
        "ple_norm": row(ple_norm),
        "w_ple_gate": w_ple_gate[i].astype(BF16),
        "w_ple": w_ple[i].astype(BF16),
    }


def _rope_rows(positions):
    inv_freq = ROPE_THETA ** (-jnp.arange(0, MLA_ROPE, 2, dtype=F32) / MLA_ROPE)
    ang = positions.astype(F32)[..., None] * inv_freq
    cos, sin = jnp.cos(ang), jnp.sin(ang)
    lead = positions.shape + (MLA_NOPE,)
    tail = positions.shape + (LANES - MLA_NOPE - MLA_ROPE,)
    cos_t = jnp.concatenate([jnp.ones(lead, F32), cos, cos, jnp.zeros(tail, F32)], axis=-1)
    sin_t = jnp.concatenate([jnp.zeros(lead, F32), -sin, sin, jnp.zeros(tail, F32)], axis=-1)
    return cos_t, sin_t


def kernel(x, p, positions, attn_norm, w_in, b_forget, q_norm, w_uq, kv_norm, w_ukv, conv_w,
           mla_out_norm, conv_out_norm, fox_out_norm, w_out, ffn_norm, w_up, ffn_conv_w,
           ffn_conv_b, w_down, ple_norm, w_ple_gate, w_ple, final_norm):
    depth = w_in.shape[0]
    cos_t, sin_t = _rope_rows(positions)
    fin = final_norm.reshape(1, D_MODEL)
    h = x
    for i in range(depth):
        w = _layer_weights(i, attn_norm, w_in, b_forget, q_norm, w_uq, kv_norm, w_ukv, conv_w,
                           mla_out_norm, conv_out_norm, fox_out_norm, w_out, ffn_norm, w_up,
                           ffn_conv_w, ffn_conv_b, w_down, ple_norm, w_ple_gate, w_ple)
        q, k, v, oc = _in_call(h, cos_t, sin_t, w)
        o = _attn_call(q, k, v)
        h = _out_call(o, oc, h, w)
        h = _ffn_call(h, w)
        h = _ple_call(h, p, i, w, fin, final=(i == depth - 1))
    return h
```

```python
import functools
import math

import jax
import jax.numpy as jnp
from jax import lax
from jax.experimental import pallas as pl
from jax.experimental.pallas import tpu as pltpu

F32 = jnp.float32
BF16 = jnp.bfloat16

D_MODEL = 1024
MLA_HEADS = 6
MLA_NOPE = 64
MLA_ROPE = 32
MLA_V = 64
Q_RANK = 256
KV_RANK = 128
FOX_HEADS = 6
FOX_DIM = 64
CONV_WIDTH = 256
CONV_K = 3
D_FF = 2816
PLE_DIM = 256
ROPE_THETA = 10000.0
EPS = 1e-6
LOG2E = math.log2(math.e)

LANES = 128
BF16_SUBLANES = 16
V7X_VMEM_BYTES = 64 * 1024 * 1024
VMEM_LIMIT = V7X_VMEM_BYTES - 8 * 1024 * 1024

HEADS = MLA_HEADS + FOX_HEADS
HW = MLA_HEADS * LANES
ATT_W = HEADS * 64

C_ZQ = 0
C_ZKV = C_ZQ + Q_RANK
C_ZR = C_ZKV + KV_RANK
C_ZRS = C_ZR + LANES
C_ZB = C_ZRS + LANES
C_ZC = C_ZB + CONV_WIDTH
C_ZH = C_ZC + CONV_WIDTH
C_FQ = C_ZH + CONV_WIDTH
C_FK = C_FQ + HW
C_FV = C_FK + HW
C_FF = C_FV + HW
N_Z = C_FF + LANES

L_QC = 64
L_KC = 67

TS_IN = 512
TQ = 512
TS_OUT = 512
TS_FFN = 512
TF_FFN = 1408
HALO = BF16_SUBLANES
TS_PLE = 512


def _rms(x, g):
    return x * lax.rsqrt(jnp.mean(x * x, axis=-1, keepdims=True) + EPS) * g


def _split3(x):
    hi = x.astype(BF16)
    r = x - hi.astype(F32)
    mid = r.astype(BF16)
    lo = (r - mid.astype(F32)).astype(BF16)
    return hi, mid, lo


def _dot(a, b):
    return jnp.dot(a, b, preferred_element_type=F32)


def _in_kernel(x_ref, g_ref, win_ref, qn_ref, wuq_ref, kvn_ref, wukv_ref, cos_ref, sin_ref,
               cw_ref, cn_ref, bf_ref, pq_ref, pk_ref, rows_ref,
               q_ref, k_ref, v_ref, oc_ref, xs_sc, cum_sc, *, ts):
    @pl.when(pl.program_id(1) == 0)
    def _():
        xs_sc[0:8, :] = jnp.zeros((8, CONV_WIDTH), F32)
        cum_sc[...] = jnp.zeros_like(cum_sc)

    a = _rms(x_ref[0], g_ref[...]).astype(BF16)

    def proj(lo, width):
        return _dot(a, win_ref[:, lo:lo + width])

    cos_t = cos_ref[0]
    sin_t = sin_ref[0]

    qn = _rms(proj(C_ZQ, Q_RANK), qn_ref[...]).astype(BF16)
    q_main = _dot(qn, wuq_ref[:, 0:HW])
    q_swap = _dot(qn, wuq_ref[:, HW:2 * HW])
    mla_scale = LOG2E * (MLA_NOPE + MLA_ROPE) ** -0.5
    kvn = _rms(proj(C_ZKV, KV_RANK), kvn_ref[...]).astype(BF16)
    k_nope = _dot(kvn, wukv_ref[:, 0:HW])
    v_mla = _dot(kvn, wukv_ref[:, HW:2 * HW])
    k_rope = proj(C_ZR, LANES) * cos_t + proj(C_ZRS, LANES) * sin_t
    for h in range(MLA_HEADS):
        sl = slice(h * LANES, (h + 1) * LANES)
        q_ref[0, h] = ((q_main[:, sl] * cos_t + q_swap[:, sl] * sin_t) * mla_scale).astype(BF16)
        k_ref[0, h] = (k_nope[:, sl] + k_rope).astype(BF16)
        v_ref[0, h] = (v_mla[:, sl] + rows_ref[0:1, sl]).astype(BF16)

    xc = proj(C_ZC, CONV_WIDTH) * proj(C_ZH, CONV_WIDTH)
    xs_sc[8:8 + ts, :] = xc
    y = (cw_ref[0:1, :] * xs_sc[6:6 + ts, :] + cw_ref[1:2, :] * xs_sc[7:7 + ts, :]
         + cw_ref[2:3, :] * xc)
    oc_ref[0] = _rms(proj(C_ZB, CONV_WIDTH) * y, cn_ref[...]).astype(BF16)
    xs_sc[0:8, :] = xs_sc[ts:ts + 8, :]

    ff = proj(C_FF, LANES) + bf_ref[...]
    log_f = jnp.minimum(ff, 0.0) - jnp.log1p(jnp.exp(-jnp.abs(ff)))
    row = lax.broadcasted_iota(jnp.int32, (ts, ts), 0)
    col = lax.broadcasted_iota(jnp.int32, (ts, ts), 1)
    tri = (col <= row).astype(BF16)
    f_hi, f_mid, f_lo = _split3(log_f)
    cum = _dot(tri, f_hi) + _dot(tri, f_mid) + _dot(tri, f_lo) + cum_sc[0:1, :]
    cum_sc[0:1, :] = cum[ts - 1:ts, :]
    c_hi, c_mid, c_lo = _split3(cum * LOG2E)
    aug_q = _dot(c_hi, pq_ref[0]) + _dot(c_mid, pq_ref[1]) + _dot(c_lo, pq_ref[2])
    aug_k = _dot(c_hi, pk_ref[0]) + _dot(c_mid, pk_ref[1]) + _dot(c_lo, pk_ref[2])

    fq = proj(C_FQ, HW)
    fk = proj(C_FK, HW)
    fv = proj(C_FV, HW)
    fox_scale = LOG2E * FOX_DIM ** -0.5
    for h in range(FOX_HEADS):
        sl = slice(h * LANES, (h + 1) * LANES)
        q_ref[0, MLA_HEADS + h] = (fq[:, sl] * fox_scale + aug_q[:, sl] + rows_ref[1:2, sl]).astype(BF16)
        k_ref[0, MLA_HEADS + h] = (fk[:, sl] + aug_k[:, sl] + rows_ref[2:3, sl]).astype(BF16)
        v_ref[0, MLA_HEADS + h] = (fv[:, sl] + rows_ref[0:1, sl]).astype(BF16)


def _in_call(h, cos_t, sin_t, w, *, ts=TS_IN):
    b, s, _ = h.shape
    const = lambda shape: pl.BlockSpec(shape, lambda bi, si: (0,) * len(shape))
    head_spec = pl.BlockSpec((1, HEADS, ts, LANES), lambda bi, si: (bi, 0, si, 0))
    head_shape = jax.ShapeDtypeStruct((b, HEADS, s, LANES), BF16)
    return pl.pallas_call(
        functools.partial(_in_kernel, ts=ts),
        grid=(b, s // ts),
        in_specs=[
            pl.BlockSpec((1, ts, D_MODEL), lambda bi, si: (bi, si, 0)),
            const((1, D_MODEL)),
            const((D_MODEL, N_Z)),
            const((1, Q_RANK)),
            const((Q_RANK, 2 * HW)),
            const((1, KV_RANK)),
            const((KV_RANK, 2 * HW)),
            pl.BlockSpec((1, ts, LANES), lambda bi, si: (bi, si, 0)),
            pl.BlockSpec((1, ts, LANES), lambda bi, si: (bi, si, 0)),
            const((CONV_K, CONV_WIDTH)),
            const((1, CONV_WIDTH)),
            const((1, LANES)),
            const((3, LANES, HW)),
            const((3, LANES, HW)),
            const((3, HW)),
        ],
        out_specs=[head_spec, head_spec, head_spec,
                   pl.BlockSpec((1, ts, CONV_WIDTH), lambda bi, si: (bi, si, 0))],
        out_shape=[head_shape, head_shape, head_shape,
                   jax.ShapeDtypeStruct((b, s, CONV_WIDTH), BF16)],
        scratch_shapes=[pltpu.VMEM((ts + 8, CONV_WIDTH), F32), pltpu.VMEM((8, LANES), F32)],
        compiler_params=pltpu.CompilerParams(
            dimension_semantics=("arbitrary", "arbitrary"), vmem_limit_bytes=VMEM_LIMIT),
        name="in_proj",
    )(h, w["attn_norm"], w["w_in"], w["q_norm"], w["w_uq"], w["kv_norm"], w["w_ukv"],
      cos_t, sin_t, w["conv_w"], w["conv_out_norm"], w["b_forget"], w["place_q"], w["place_k"],
      w["rows"])


def _attn_kernel(q_ref, k_ref, v_ref, o_ref, s_sc, m_sc, acc_sc, *, t):
    i = pl.program_id(2)
    m_sc[...] = jnp.full_like(m_sc, -jnp.inf)
    acc_sc[...] = jnp.zeros_like(acc_sc)

    def scores(hh, j, slot):
        off = pl.multiple_of(j * t, t)
        s_sc[slot, hh] = lax.dot_general(q_ref[0, hh], k_ref[0, hh, pl.ds(off, t), :],
                                         (((1,), (1,)), ((), ())), preferred_element_type=F32)

    def consume(hh, j, slot, diag):
        off = pl.multiple_of(j * t, t)
        s = s_sc[slot, hh]
        if diag:
            row = lax.broadcasted_iota(jnp.int32, (t, t), 0)
            col = lax.broadcasted_iota(jnp.int32, (t, t), 1)
            s = jnp.where(col <= row, s, -jnp.inf)
        m_prev = m_sc[hh]
        m_new = jnp.maximum(m_prev, jnp.max(s, axis=-1, keepdims=True))
        alpha = jnp.exp2(m_prev - m_new)
        p = jnp.exp2(s - jnp.tile(m_new, (1, t // LANES))).astype(BF16)
        acc_sc[hh] = alpha * acc_sc[hh] + _dot(p, v_ref[0, hh, pl.ds(off, t), :])
        m_sc[hh] = m_new

    for hh in range(2):
        scores(hh, 0, 0)

    def advance(j, slot, diag):
        if not diag:
            for hh in range(2):
                scores(hh, j + 1, 1 - slot)
        for hh in range(2):
            consume(hh, j, slot, diag)

    def body(jj, carry):
        advance(2 * jj, 0, False)
        advance(2 * jj + 1, 1, False)
        return carry

    lax.fori_loop(0, i // 2, body, 0)

    @pl.when(i % 2 == 0)
    def _():
        advance(i, 0, True)

    @pl.when(i % 2 == 1)
    def _():
        advance(i - 1, 0, False)
        advance(i, 1, True)

    lane = lax.broadcasted_iota(jnp.int32, (t, LANES), 1)
    acc0 = acc_sc[0]
    acc1 = acc_sc[1]
    o_ref[0] = jnp.where(lane < 64, acc0 / acc0[:, 64:65], acc1 / acc1[:, 0:1])


def _attn_call(q, k, v, *, tq=TQ):
    b, _, s, _ = q.shape
    return pl.pallas_call(
        functools.partial(_attn_kernel, t=tq),
        grid=(b, HEADS // 2, s // tq),
        in_specs=[
            pl.BlockSpec((1, 2, tq, LANES), lambda bi, g, i: (bi, g, i, 0)),
            pl.BlockSpec((1, 2, s, LANES), lambda bi, g, i: (bi, g, 0, 0)),
            pl.BlockSpec((1, 2, s, LANES), lambda bi, g, i: (bi, g, 0, 0)),
        ],
        out_specs=pl.BlockSpec((1, tq, LANES), lambda bi, g, i: (bi, i, g)),
        out_shape=jax.ShapeDtypeStruct((b, s, ATT_W), F32),
        scratch_shapes=[pltpu.VMEM((2, 2, tq, tq), F32),
                        pltpu.VMEM((2, tq, LANES), F32), pltpu.VMEM((2, tq, LANES), F32)],
        compiler_params=pltpu.CompilerParams(
            dimension_semantics=("arbitrary", "arbitrary", "arbitrary"),
            vmem_limit_bytes=VMEM_LIMIT),
        name="attention",
    )(q, k, v)


def _out_kernel(o_ref, oc_ref, h_ref, gm_ref, gf_ref, wout_ref, out_ref):
    o = o_ref[0]
    half = ATT_W // 2
    mixed = jnp.concatenate([
        _rms(o[:, :half], gm_ref[...]).astype(BF16),
        oc_ref[0],
        _rms(o[:, half:], gf_ref[...]).astype(BF16)], axis=-1)
    out_ref[0] = h_ref[0] + _dot(mixed, wout_ref[...])


def _out_call(o, oc, h, w, *, ts=TS_OUT):
    b, s, _ = h.shape
    const = lambda shape: pl.BlockSpec(shape, lambda bi, si: (0,) * len(shape))
    row = lambda width: pl.BlockSpec((1, ts, width), lambda bi, si: (bi, si, 0))
    return pl.pallas_call(
        _out_kernel,
        grid=(b, s // ts),
        in_specs=[row(ATT_W), row(CONV_WIDTH), row(D_MODEL),
                  const((1, ATT_W // 2)), const((1, ATT_W // 2)), const((D_MODEL, D_MODEL))],
        out_specs=row(D_MODEL),
        out_shape=jax.ShapeDtypeStruct(h.shape, F32),
        compiler_params=pltpu.CompilerParams(
            dimension_semantics=("arbitrary", "arbitrary"), vmem_limit_bytes=VMEM_LIMIT),
        name="out_proj",
    )(o, oc, h, w["mla_out_norm"], w["fox_out_norm"], w["w_out"])


def _ffn_kernel(hp_ref, h_ref, g_ref, wg_ref, wv_ref, cwg_ref, cwv_ref, cbg_ref, cbv_ref, wd_ref,
                out_ref, m_sc, *, ts):
    c = pl.program_id(2)

    @pl.when(c == 0)
    def _():
        keep = (pl.program_id(1) > 0).astype(F32)
        m_sc[0:HALO, :] = _rms(hp_ref[0] * keep, g_ref[...]).astype(BF16)
        m_sc[HALO:HALO + ts, :] = _rms(h_ref[0], g_ref[...]).astype(BF16)

    m = m_sc[...]

    def conv(u, cw_ref, cb_ref):
        return (cw_ref[0:1, :] * u[HALO - 2:HALO - 2 + ts, :]
                + cw_ref[1:2, :] * u[HALO - 1:HALO - 1 + ts, :]
                + cw_ref[2:3, :] * u[HALO:HALO + ts, :] + cb_ref[...])

    gate = conv(_dot(m, wg_ref[...]), cwg_ref, cbg_ref)
    val = conv(_dot(m, wv_ref[...]), cwv_ref, cbv_ref)
    act = (gate * jax.nn.sigmoid(gate) * val).astype(BF16)
    part = _dot(act, wd_ref[...])

    @pl.when(c == 0)
    def _():
        out_ref[0] = h_ref[0] + part

    @pl.when(c > 0)
    def _():
        out_ref[0] += part


def _ffn_call(h, w, *, ts=TS_FFN, tf=TF_FFN):
    b, s, _ = h.shape
    nf = D_FF // tf
    halo_blocks = ts // HALO
    return pl.pallas_call(
        functools.partial(_ffn_kernel, ts=ts),
        grid=(b, s // ts, nf),
        in_specs=[
            pl.BlockSpec((1, HALO, D_MODEL),
                         lambda bi, si, c: (bi, jnp.maximum(si * halo_blocks - 1, 0), 0)),
            pl.BlockSpec((1, ts, D_MODEL), lambda bi, si, c: (bi, si, 0)),
            pl.BlockSpec((1, D_MODEL), lambda bi, si, c: (0, 0)),
            pl.BlockSpec((D_MODEL, tf), lambda bi, si, c: (0, c)),
            pl.BlockSpec((D_MODEL, tf), lambda bi, si, c: (0, nf + c)),
            pl.BlockSpec((CONV_K, tf), lambda bi, si, c: (0, c)),
            pl.BlockSpec((CONV_K, tf), lambda bi, si, c: (0, nf + c)),
            pl.BlockSpec((1, tf), lambda bi, si, c: (0, c)),
            pl.BlockSpec((1, tf), lambda bi, si, c: (0, nf + c)),
            pl.BlockSpec((tf, D_MODEL), lambda bi, si, c: (c, 0)),
        ],
        out_specs=pl.BlockSpec((1, ts, D_MODEL), lambda bi, si, c: (bi, si, 0)),
        out_shape=jax.ShapeDtypeStruct(h.shape, F32),
        scratch_shapes=[pltpu.VMEM((ts + HALO, D_MODEL), BF16)],
        compiler_params=pltpu.CompilerParams(
            dimension_semantics=("arbitrary", "arbitrary", "arbitrary"),
            vmem_limit_bytes=VMEM_LIMIT),
        name="ffn",
    )(h, h, w["ffn_norm"], w["w_up"], w["w_up"], w["ffn_conv_w"], w["ffn_conv_w"],
      w["ffn_conv_b"], w["ffn_conv_b"], w["w_down"])


def _ple_kernel(h_ref, p_ref, g_ref, wpg_ref, wple_ref, fg_ref, out_ref, *, final):
    h = h_ref[0]
    gate = jax.nn.sigmoid(_dot(_rms(h, g_ref[...]).astype(BF16), wpg_ref[...]))
    out = h + gate * _dot(p_ref[0, 0].astype(BF16), wple_ref[...])
    if final:
        out = _rms(out, fg_ref[...])
    out_ref[0] = out


def _ple_call(h, p, layer, w, final_norm, *, final, ts=TS_PLE):
    b, s, _ = h.shape
    const = lambda shape: pl.BlockSpec(shape, lambda bi, si: (0,) * len(shape))
    row = pl.BlockSpec((1, ts, D_MODEL), lambda bi, si: (bi, si, 0))
    return pl.pallas_call(
        functools.partial(_ple_kernel, final=final),
        grid=(b, s // ts),
        in_specs=[row,
                  pl.BlockSpec((1, 1, ts, PLE_DIM), lambda bi, si: (layer, bi, si, 0)),
                  const((1, D_MODEL)), const((D_MODEL, D_MODEL)), const((PLE_DIM, D_MODEL)),
                  const((1, D_MODEL))],
        out_specs=row,
        out_shape=jax.ShapeDtypeStruct(h.shape, F32),
        compiler_params=pltpu.CompilerParams(
            dimension_semantics=("arbitrary", "arbitrary"), vmem_limit_bytes=VMEM_LIMIT),
        name="ple",
    )(h, p, w["ple_norm"], w["w_ple_gate"], w["w_ple"], final_norm)


def _head_rows(w, width, offsets):
    kdim = w.shape[0]
    w = w.reshape(kdim, MLA_HEADS, width)
    out = jnp.zeros((kdim, MLA_HEADS, LANES), w.dtype)
    for h, off in enumerate(offsets):
        out = out.at[:, h, off:off + width].set(w[:, h])
    return out.reshape(kdim, HW)


V_OFFSETS = tuple(0 if h % 2 == 0 else 64 for h in range(MLA_HEADS))


def _placement(lane0, sign):
    mats = []
    for t in range(3):
        m = jnp.zeros((LANES, MLA_HEADS, LANES), F32)
        for h in range(FOX_HEADS):
            m = m.at[h, h, lane0 + t].set(sign)
        mats.append(m.reshape(LANES, HW))
    return jnp.stack(mats).astype(BF16)


def _const_rows():
    rows = jnp.zeros((3, MLA_HEADS, LANES), F32)
    for h in range(MLA_HEADS):
        rows = rows.at[0, h, 64 if h % 2 == 0 else 0].set(1.0)
    rows = rows.at[1, :, L_KC:L_KC + 3].set(1.0)
    rows = rows.at[2, :, L_QC:L_QC + 3].set(1.0)
    return rows.reshape(3, HW)


def _layer_weights(i, attn_norm, w_in, b_forget, q_norm, w_uq, kv_norm, w_ukv, conv_w,
                   mla_out_norm, conv_out_norm, fox_out_norm, w_out, ffn_norm, w_up,
                   ffn_conv_w, ffn_conv_b, w_down, ple_norm, w_ple_gate, w_ple):
    sizes = (Q_RANK, KV_RANK, MLA_ROPE, CONV_WIDTH, CONV_WIDTH, CONV_WIDTH,
             FOX_HEADS * FOX_DIM, FOX_HEADS * FOX_DIM, FOX_HEADS * FOX_DIM, FOX_HEADS)
    splits = [sum(sizes[:j + 1]) for j in range(len(sizes) - 1)]
    wq, wkv, wr, wb, wc, wh, wfq, wfk, wfv, wff = jnp.split(w_in[i], splits, axis=-1)
    half = MLA_ROPE // 2
    zeros = lambda n: jnp.zeros((D_MODEL, n), F32)
    wr_swapped = jnp.concatenate([wr[:, half:], wr[:, :half]], axis=-1)
    w_in_p = jnp.concatenate([
        wq, wkv,
        zeros(MLA_NOPE), wr, zeros(LANES - MLA_NOPE - MLA_ROPE),
        zeros(MLA_NOPE), wr_swapped, zeros(LANES - MLA_NOPE - MLA_ROPE),
        wb, wc, wh,
        _head_rows(wfq, FOX_DIM, (0,) * FOX_HEADS),
        _head_rows(wfk, FOX_DIM, (0,) * FOX_HEADS),
        _head_rows(wfv, FOX_DIM, V_OFFSETS),
        wff, zeros(LANES - FOX_HEADS)], axis=-1).astype(BF16)

    uq = w_uq[i].reshape(Q_RANK, MLA_HEADS, MLA_NOPE + MLA_ROPE)
    uq_main = _head_rows(uq.reshape(Q_RANK, -1), MLA_NOPE + MLA_ROPE, (0,) * MLA_HEADS)
    uq_rot = jnp.concatenate([uq[..., MLA_NOPE + half:], uq[..., MLA_NOPE:MLA_NOPE + half]], axis=-1)
    uq_swap = _head_rows(uq_rot.reshape(Q_RANK, -1), MLA_ROPE, (MLA_NOPE,) * MLA_HEADS)
    ukv = w_ukv[i].reshape(KV_RANK, MLA_HEADS, MLA_NOPE + MLA_V)
    uk = _head_rows(ukv[..., :MLA_NOPE].reshape(KV_RANK, -1), MLA_NOPE, (0,) * MLA_HEADS)
    uv = _head_rows(ukv[..., MLA_NOPE:].reshape(KV_RANK, -1), MLA_V, V_OFFSETS)

    row = lambda v: v[i].reshape(1, -1)
    return {
        "attn_norm": row(attn_norm),
        "w_in": w_in_p,
        "q_norm": row(q_norm),
        "w_uq": jnp.concatenate([uq_main, uq_swap], axis=-1).astype(BF16),
        "kv_norm": row(kv_norm),
        "w_ukv": jnp.concatenate([uk, uv], axis=-1).astype(BF16),
        "conv_w": conv_w[i],
        "conv_out_norm": row(conv_out_norm),
        "b_forget": jnp.pad(b_forget[i], (0, LANES - FOX_HEADS)).reshape(1, LANES),
        "place_q": _placement(L_QC, 1.0),
        "place_k": _placement(L_KC, -1.0),
        "rows": _const_rows(),
        "mla_out_norm": row(mla_out_norm),
        "fox_out_norm": row(fox_out_norm),
        "w_out": w_out[i].astype(BF16),
        "ffn_norm": row(ffn_norm),
        "w_up": w_up[i].astype(BF16),
        "ffn_conv_w": ffn_conv_w[i],
        "ffn_conv_b": row(ffn_conv_b),
        "w_down": w_down[i].astype(BF16),
        "ple_norm": row(ple_norm),
        "w_ple_gate": w_ple_gate[i].astype(BF16),
        "w_ple": w_ple[i].astype(BF16),
    }


def _rope_rows(positions):
    inv_freq = ROPE_THETA ** (-jnp.arange(0, MLA_ROPE, 2, dtype=F32) / MLA_ROPE)
    ang = positions.astype(F32)[..., None] * inv_freq
    cos, sin = jnp.cos(ang), jnp.sin(ang)
    lead = positions.shape + (MLA_NOPE,)
    tail = positions.shape + (LANES - MLA_NOPE - MLA_ROPE,)
    cos_t = jnp.concatenate([jnp.ones(lead, F32), cos, cos, jnp.zeros(tail, F32)], axis=-1)
    sin_t = jnp.concatenate([jnp.zeros(lead, F32), -sin, sin, jnp.zeros(tail, F32)], axis=-1)
    return cos_t, sin_t


def kernel(x, p, positions, attn_norm, w_in, b_forget, q_norm, w_uq, kv_norm, w_ukv, conv_w,
           mla_out_norm, conv_out_norm, fox_out_norm, w_out, ffn_norm, w_up, ffn_conv_w,
           ffn_conv_b, w_down, ple_norm, w_ple_gate, w_ple, final_norm):
    depth = w_in.shape[0]
    cos_t, sin_t = _rope_rows(positions)
    fin = final_norm.reshape(1, D_MODEL)
    h = x
    for i in range(depth):
        w = _layer_weights(i, attn_norm, w_in, b_forget, q_norm, w_uq, kv_norm, w_ukv, conv_w,
                           mla_out_norm, conv_out_norm, fox_out_norm, w_out, ffn_norm, w_up,
                           ffn_conv_w, ffn_conv_b, w_down, ple_norm, w_ple_gate, w_ple)
        q, k, v, oc = _in_call(h, cos_t, sin_t, w)
        o = _attn_call(q, k, v)
        h = _out_call(o, oc, h, w)
        h = _ffn_call(h, w)
        h = _ple_call(h, p, i, w, fin, final=(i == depth - 1))
    return h
```

```python
import functools
import math

import jax
import jax.numpy as jnp
from jax import lax
from jax.experimental import pallas as pl
from jax.experimental.pallas import tpu as pltpu

F32 = jnp.float32
BF16 = jnp.bfloat16

D_MODEL = 1024
MLA_HEADS = 6
MLA_NOPE = 64
MLA_ROPE = 32
MLA_V = 64
Q_RANK = 256
KV_RANK = 128
FOX_HEADS = 6
FOX_DIM = 64
CONV_WIDTH = 256
CONV_K = 3
D_FF = 2816
PLE_DIM = 256
ROPE_THETA = 10000.0
EPS = 1e-6
LOG2E = math.log2(math.e)

LANES = 128
BF16_SUBLANES = 16
V7X_VMEM_BYTES = 64 * 1024 * 1024
VMEM_LIMIT = V7X_VMEM_BYTES - 8 * 1024 * 1024

HEADS = MLA_HEADS + FOX_HEADS
HW = MLA_HEADS * LANES
ATT_W = HEADS * 64

C_ZQ = 0
C_ZKV = C_ZQ + Q_RANK
C_ZR = C_ZKV + KV_RANK
C_ZRS = C_ZR + LANES
C_ZB = C_ZRS + LANES
C_ZC = C_ZB + CONV_WIDTH
C_ZH = C_ZC + CONV_WIDTH
C_FQ = C_ZH + CONV_WIDTH
C_FK = C_FQ + HW
C_FF = C_FK + HW
N_Z = C_FF + LANES

L_QC = 64
L_KC = 67

TS_IN = 512
T_ATT = 512
MASK_BIAS = -1e30
TS_OUT = 512
TS_FFN = 512
TF_FFN = 1408
HALO = BF16_SUBLANES
TS_PLE = 512


def _rms(x, g):
    return x * lax.rsqrt(jnp.mean(x * x, axis=-1, keepdims=True) + EPS) * g


def _split3(x):
    hi = x.astype(BF16)
    r = x - hi.astype(F32)
    mid = r.astype(BF16)
    lo = (r - mid.astype(F32)).astype(BF16)
    return hi, mid, lo


def _dot(a, b):
    return jnp.dot(a, b, preferred_element_type=F32)


def _dot_nt(a, b):
    return lax.dot_general(a, b, (((1,), (1,)), ((), ())), preferred_element_type=F32)


def _in_kernel(x_ref, g_ref, win_ref, qn_ref, wuq_ref, kvn_ref, wuk_ref, wuvt_ref, wfvt_ref,
               cos_ref, sin_ref, cw_ref, cn_ref, bf_ref, pq_ref, pk_ref, rows_ref, vones_ref,
               q_ref, k_ref, vt_ref, oc_ref, xs_sc, cum_sc, *, ts):
    @pl.when(pl.program_id(1) == 0)
    def _():
        xs_sc[0:8, :] = jnp.zeros((8, CONV_WIDTH), F32)
        cum_sc[...] = jnp.zeros_like(cum_sc)

    a = _rms(x_ref[0], g_ref[...]).astype(BF16)

    def proj(lo, width):
        return _dot(a, win_ref[:, lo:lo + width])

    cos_t = cos_ref[0]
    sin_t = sin_ref[0]

    qn = _rms(proj(C_ZQ, Q_RANK), qn_ref[...]).astype(BF16)
    q_main = _dot(qn, wuq_ref[:, 0:HW])
    q_swap = _dot(qn, wuq_ref[:, HW:2 * HW])
    mla_scale = LOG2E * (MLA_NOPE + MLA_ROPE) ** -0.5
    kvn = _rms(proj(C_ZKV, KV_RANK), kvn_ref[...]).astype(BF16)
    k_nope = _dot(kvn, wuk_ref[...])
    v_ones = jnp.tile(vones_ref[...], (1, ts // LANES))
    v_mla_t = _dot_nt(wuvt_ref[...], kvn) + v_ones
    k_rope = proj(C_ZR, LANES) * cos_t + proj(C_ZRS, LANES) * sin_t
    for h in range(MLA_HEADS):
        sl = slice(h * LANES, (h + 1) * LANES)
        q_ref[0, h] = ((q_main[:, sl] * cos_t + q_swap[:, sl] * sin_t) * mla_scale).astype(BF16)
        k_ref[0, h] = (k_nope[:, sl] + k_rope).astype(BF16)
        vt_ref[0, h, 0] = v_mla_t[sl, :].astype(BF16)

    xc = proj(C_ZC, CONV_WIDTH) * proj(C_ZH, CONV_WIDTH)
    xs_sc[8:8 + ts, :] = xc
    y = (cw_ref[0:1, :] * xs_sc[6:6 + ts, :] + cw_ref[1:2, :] * xs_sc[7:7 + ts, :]
         + cw_ref[2:3, :] * xc)
    oc_ref[0] = _rms(proj(C_ZB, CONV_WIDTH) * y, cn_ref[...]).astype(BF16)
    xs_sc[0:8, :] = xs_sc[ts:ts + 8, :]

    ff = proj(C_FF, LANES) + bf_ref[...]
    log_f = jnp.minimum(ff, 0.0) - jnp.log1p(jnp.exp(-jnp.abs(ff)))
    row = lax.broadcasted_iota(jnp.int32, (ts, ts), 0)
    col = lax.broadcasted_iota(jnp.int32, (ts, ts), 1)
    tri = (col <= row).astype(BF16)
    f_hi, f_mid, f_lo = _split3(log_f)
    cum = _dot(tri, f_hi) + _dot(tri, f_mid) + _dot(tri, f_lo) + cum_sc[0:1, :]
    cum_sc[0:1, :] = cum[ts - 1:ts, :]
    c_hi, c_mid, c_lo = _split3(cum * LOG2E)
    aug_q = _dot(c_hi, pq_ref[0]) + _dot(c_mid, pq_ref[1]) + _dot(c_lo, pq_ref[2])
    aug_k = _dot(c_hi, pk_ref[0]) + _dot(c_mid, pk_ref[1]) + _dot(c_lo, pk_ref[2])

    fq = proj(C_FQ, HW)
    fk = proj(C_FK, HW)
    fv_t = _dot_nt(wfvt_ref[...], a) + v_ones
    fox_scale = LOG2E * FOX_DIM ** -0.5
    for h in range(FOX_HEADS):
        sl = slice(h * LANES, (h + 1) * LANES)
        q_ref[0, MLA_HEADS + h] = (fq[:, sl] * fox_scale + aug_q[:, sl] + rows_ref[0:1, sl]).astype(BF16)
        k_ref[0, MLA_HEADS + h] = (fk[:, sl] + aug_k[:, sl] + rows_ref[1:2, sl]).astype(BF16)
        vt_ref[0, MLA_HEADS + h, 0] = fv_t[sl, :].astype(BF16)


def _in_call(h, cos_t, sin_t, w, *, ts=TS_IN):
    b, s, _ = h.shape
    const = lambda shape: pl.BlockSpec(shape, lambda bi, si: (0,) * len(shape))
    assert ts == T_ATT
    head_spec = pl.BlockSpec((1, HEADS, ts, LANES), lambda bi, si: (bi, 0, si, 0))
    head_shape = jax.ShapeDtypeStruct((b, HEADS, s, LANES), BF16)
    vt_spec = pl.BlockSpec((1, HEADS, 1, LANES, ts), lambda bi, si: (bi, 0, si, 0, 0))
    vt_shape = jax.ShapeDtypeStruct((b, HEADS, s // ts, LANES, ts), BF16)
    return pl.pallas_call(
        functools.partial(_in_kernel, ts=ts),
        grid=(b, s // ts),
        in_specs=[
            pl.BlockSpec((1, ts, D_MODEL), lambda bi, si: (bi, si, 0)),
            const((1, D_MODEL)),
            const((D_MODEL, N_Z)),
            const((1, Q_RANK)),
            const((Q_RANK, 2 * HW)),
            const((1, KV_RANK)),
            const((KV_RANK, HW)),
            const((HW, KV_RANK)),
            const((HW, D_MODEL)),
            pl.BlockSpec((1, ts, LANES), lambda bi, si: (bi, si, 0)),
            pl.BlockSpec((1, ts, LANES), lambda bi, si: (bi, si, 0)),
            const((CONV_K, CONV_WIDTH)),
            const((1, CONV_WIDTH)),
            const((1, LANES)),
            const((3, LANES, HW)),
            const((3, LANES, HW)),
            const((2, HW)),
            const((HW, LANES)),
        ],
        out_specs=[head_spec, head_spec, vt_spec,
                   pl.BlockSpec((1, ts, CONV_WIDTH), lambda bi, si: (bi, si, 0))],
        out_shape=[head_shape, head_shape, vt_shape,
                   jax.ShapeDtypeStruct((b, s, CONV_WIDTH), BF16)],
        scratch_shapes=[pltpu.VMEM((ts + 8, CONV_WIDTH), F32), pltpu.VMEM((8, LANES), F32)],
        compiler_params=pltpu.CompilerParams(
            dimension_semantics=("arbitrary", "arbitrary"), vmem_limit_bytes=VMEM_LIMIT),
        name="in_proj",
    )(h, w["attn_norm"], w["w_in"], w["q_norm"], w["w_uq"], w["kv_norm"], w["w_uk"], w["w_uv_t"],
      w["w_fv_t"], cos_t, sin_t, w["conv_w"], w["conv_out_norm"], w["b_forget"], w["place_q"],
      w["place_k"], w["rows"], w["v_ones"])


def _attn_kernel(q_ref, k_ref, vt_ref, bias_ref, o_ref, s_sc, m_sc, acc_sc, *, t, nb):
    m_sc[...] = jnp.full_like(m_sc, -jnp.inf)
    acc_sc[...] = jnp.zeros_like(acc_sc)
    row = lax.broadcasted_iota(jnp.int32, (LANES, t), 0)

    def chunk(ref, hh, j):
        return ref[0, hh, pl.ds(pl.multiple_of(j * t, t), t), :]

    def scores(hh, i, j, slot):
        s_t = _dot_nt(chunk(k_ref, hh, j), chunk(q_ref, hh, i))
        s_sc[slot, hh] = s_t + bias_ref[(i == j).astype(jnp.int32)]

    def consume(hh, j, slot):
        m_prev = jnp.where(j == 0, -jnp.inf, m_sc[hh])
        m_new = jnp.maximum(m_prev, jnp.max(s_sc[slot, hh], axis=0, keepdims=True))
        alpha = jnp.exp2(m_prev - m_new)
        p_t = jnp.exp2(s_sc[slot, hh] - m_new).astype(BF16)
        acc_sc[slot, hh] = alpha * acc_sc[1 - slot, hh] + _dot(vt_ref[0, hh, j], p_t)
        m_sc[hh] = m_new

    def advance(i, j, slot):
        last = j == i
        ni = jnp.where(last, jnp.minimum(i + 1, nb - 1), i)
        nj = jnp.where(last, 0, j + 1)
        for hh in range(2):
            scores(hh, ni, nj, 1 - slot)
        for hh in range(2):
            consume(hh, j, slot)
        return ni, nj

    def finish(i, slot):
        acc0 = acc_sc[slot, 0]
        acc1 = acc_sc[slot, 1]
        o_t = jnp.where(row < 64, acc0 / acc0[64:65, :], acc1 / acc1[0:1, :])
        o_ref[0, pl.ds(pl.multiple_of(i * t, t), t), :] = o_t.T

    first = (jnp.int32(0), jnp.int32(0))
    for hh in range(2):
        scores(hh, *first, 0)

    def body(_, unit0):
        unit1 = advance(*unit0, 0)
        unit2 = advance(*unit1, 1)
        for slot, (i, j) in enumerate((unit0, unit1)):
            pl.when(i == j)(functools.partial(finish, i, slot))
        return unit2

    n_units = nb * (nb + 1) // 2
    assert n_units % 2 == 0
    lax.fori_loop(0, n_units // 2, body, first)


def _attn_call(q, k, vt, *, t=T_ATT):
    b, _, s, _ = q.shape
    key = lax.broadcasted_iota(jnp.int32, (t, t), 0)
    query = lax.broadcasted_iota(jnp.int32, (t, t), 1)
    bias = jnp.stack([jnp.zeros((t, t), F32), jnp.where(key <= query, 0.0, MASK_BIAS).astype(F32)])
    seq = pl.BlockSpec((1, 2, s, LANES), lambda bi, g: (bi, g, 0, 0))
    return pl.pallas_call(
        functools.partial(_attn_kernel, t=t, nb=s // t),
        grid=(b, HEADS // 2),
        in_specs=[seq, seq,
                  pl.BlockSpec((1, 2, s // t, LANES, t), lambda bi, g: (bi, g, 0, 0, 0)),
                  pl.BlockSpec((2, t, t), lambda bi, g: (0, 0, 0))],
        out_specs=pl.BlockSpec((1, s, LANES), lambda bi, g: (bi, 0, g)),
        out_shape=jax.ShapeDtypeStruct((b, s, ATT_W), F32),
        scratch_shapes=[pltpu.VMEM((2, 2, t, t), F32), pltpu.VMEM((2, 1, t), F32),
                        pltpu.VMEM((2, 2, LANES, t), F32)],
        compiler_params=pltpu.CompilerParams(
            dimension_semantics=("arbitrary", "arbitrary"), vmem_limit_bytes=VMEM_LIMIT),
        name="attention",
    )(q, k, vt, bias)


def _out_kernel(o_ref, oc_ref, h_ref, gm_ref, gf_ref, wout_ref, out_ref):
    o = o_ref[0]
    half = ATT_W // 2
    mixed = jnp.concatenate([
        _rms(o[:, :half], gm_ref[...]).astype(BF16),
        oc_ref[0],
        _rms(o[:, half:], gf_ref[...]).astype(BF16)], axis=-1)
    out_ref[0] = h_ref[0] + _dot(mixed, wout_ref[...])


def _out_call(o, oc, h, w, *, ts=TS_OUT):
    b, s, _ = h.shape
    const = lambda shape: pl.BlockSpec(shape, lambda bi, si: (0,) * len(shape))
    row = lambda width: pl.BlockSpec((1, ts, width), lambda bi, si: (bi, si, 0))
    return pl.pallas_call(
        _out_kernel,
        grid=(b, s // ts),
        in_specs=[row(ATT_W), row(CONV_WIDTH), row(D_MODEL),
                  const((1, ATT_W // 2)), const((1, ATT_W // 2)), const((D_MODEL, D_MODEL))],
        out_specs=row(D_MODEL),
        out_shape=jax.ShapeDtypeStruct(h.shape, F32),
        compiler_params=pltpu.CompilerParams(
            dimension_semantics=("arbitrary", "arbitrary"), vmem_limit_bytes=VMEM_LIMIT),
        name="out_proj",
    )(o, oc, h, w["mla_out_norm"], w["fox_out_norm"], w["w_out"])


def _ffn_kernel(hp_ref, h_ref, g_ref, wg_ref, wv_ref, cwg_ref, cwv_ref, cbg_ref, cbv_ref, wd_ref,
                out_ref, m_sc, *, ts):
    c = pl.program_id(2)

    @pl.when(c == 0)
    def _():
        keep = (pl.program_id(1) > 0).astype(F32)
        m_sc[0:HALO, :] = _rms(hp_ref[0] * keep, g_ref[...]).astype(BF16)
        m_sc[HALO:HALO + ts, :] = _rms(h_ref[0], g_ref[...]).astype(BF16)

    m = m_sc[...]

    def conv(u, cw_ref, cb_ref):
        return (cw_ref[0:1, :] * u[HALO - 2:HALO - 2 + ts, :]
                + cw_ref[1:2, :] * u[HALO - 1:HALO - 1 + ts, :]
                + cw_ref[2:3, :] * u[HALO:HALO + ts, :] + cb_ref[...])

    gate = conv(_dot(m, wg_ref[...]), cwg_ref, cbg_ref)
    val = conv(_dot(m, wv_ref[...]), cwv_ref, cbv_ref)
    act = (gate * jax.nn.sigmoid(gate) * val).astype(BF16)
    part = _dot(act, wd_ref[...])

    @pl.when(c == 0)
    def _():
        out_ref[0] = h_ref[0] + part

    @pl.when(c > 0)
    def _():
        out_ref[0] += part


def _ffn_call(h, w, *, ts=TS_FFN, tf=TF_FFN):
    b, s, _ = h.shape
    nf = D_FF // tf
    halo_blocks = ts // HALO
    return pl.pallas_call(
        functools.partial(_ffn_kernel, ts=ts),
        grid=(b, s // ts, nf),
        in_specs=[
            pl.BlockSpec((1, HALO, D_MODEL),
                         lambda bi, si, c: (bi, jnp.maximum(si * halo_blocks - 1, 0), 0)),
            pl.BlockSpec((1, ts, D_MODEL), lambda bi, si, c: (bi, si, 0)),
            pl.BlockSpec((1, D_MODEL), lambda bi, si, c: (0, 0)),
            pl.BlockSpec((D_MODEL, tf), lambda bi, si, c: (0, c)),
            pl.BlockSpec((D_MODEL, tf), lambda bi, si, c: (0, nf + c)),
            pl.BlockSpec((CONV_K, tf), lambda bi, si, c: (0, c)),
            pl.BlockSpec((CONV_K, tf), lambda bi, si, c: (0, nf + c)),
            pl.BlockSpec((1, tf), lambda bi, si, c: (0, c)),
            pl.BlockSpec((1, tf), lambda bi, si, c: (0, nf + c)),
            pl.BlockSpec((tf, D_MODEL), lambda bi, si, c: (c, 0)),
        ],
        out_specs=pl.BlockSpec((1, ts, D_MODEL), lambda bi, si, c: (bi, si, 0)),
        out_shape=jax.ShapeDtypeStruct(h.shape, F32),
        scratch_shapes=[pltpu.VMEM((ts + HALO, D_MODEL), BF16)],
        compiler_params=pltpu.CompilerParams(
            dimension_semantics=("arbitrary", "arbitrary", "arbitrary"),
            vmem_limit_bytes=VMEM_LIMIT),
        name="ffn",
    )(h, h, w["ffn_norm"], w["w_up"], w["w_up"], w["ffn_conv_w"], w["ffn_conv_w"],
      w["ffn_conv_b"], w["ffn_conv_b"], w["w_down"])


def _ple_kernel(h_ref, p_ref, g_ref, wpg_ref, wple_ref, fg_ref, out_ref, *, final):
    h = h_ref[0]
    gate = jax.nn.sigmoid(_dot(_rms(h, g_ref[...]).astype(BF16), wpg_ref[...]))
    out = h + gate * _dot(p_ref[0, 0].astype(BF16), wple_ref[...])
    if final:
        out = _rms(out, fg_ref[...])
    out_ref[0] = out


def _ple_call(h, p, layer, w, final_norm, *, final, ts=TS_PLE):
    b, s, _ = h.shape
    const = lambda shape: pl.BlockSpec(shape, lambda bi, si: (0,) * len(shape))
    row = pl.BlockSpec((1, ts, D_MODEL), lambda bi, si: (bi, si, 0))
    return pl.pallas_call(
        functools.partial(_ple_kernel, final=final),
        grid=(b, s // ts),
        in_specs=[row,
                  pl.BlockSpec((1, 1, ts, PLE_DIM), lambda bi, si: (layer, bi, si, 0)),
                  const((1, D_MODEL)), const((D_MODEL, D_MODEL)), const((PLE_DIM, D_MODEL)),
                  const((1, D_MODEL))],
        out_specs=row,
        out_shape=jax.ShapeDtypeStruct(h.shape, F32),
        compiler_params=pltpu.CompilerParams(
            dimension_semantics=("arbitrary", "arbitrary"), vmem_limit_bytes=VMEM_LIMIT),
        name="ple",
    )(h, p, w["ple_norm"], w["w_ple_gate"], w["w_ple"], final_norm)


def _head_rows(w, width, offsets):
    kdim = w.shape[0]
    w = w.reshape(kdim, MLA_HEADS, width)
    out = jnp.zeros((kdim, MLA_HEADS, LANES), w.dtype)
    for h, off in enumerate(offsets):
        out = out.at[:, h, off:off + width].set(w[:, h])
    return out.reshape(kdim, HW)


V_OFFSETS = tuple(0 if h % 2 == 0 else 64 for h in range(MLA_HEADS))


def _placement(lane0, sign):
    mats = []
    for t in range(3):
        m = jnp.zeros((LANES, MLA_HEADS, LANES), F32)
        for h in range(FOX_HEADS):
            m = m.at[h, h, lane0 + t].set(sign)
        mats.append(m.reshape(LANES, HW))
    return jnp.stack(mats).astype(BF16)


def _const_rows():
    rows = jnp.zeros((2, MLA_HEADS, LANES), F32)
    rows = rows.at[0, :, L_KC:L_KC + 3].set(1.0)
    rows = rows.at[1, :, L_QC:L_QC + 3].set(1.0)
    return rows.reshape(2, HW)


def _v_ones():
    col = jnp.zeros((MLA_HEADS, LANES), F32)
    for h in range(MLA_HEADS):
        col = col.at[h, 64 if h % 2 == 0 else 0].set(1.0)
    return jnp.broadcast_to(col.reshape(HW, 1), (HW, LANES))


def _layer_weights(i, attn_norm, w_in, b_forget, q_norm, w_uq, kv_norm, w_ukv, conv_w,
                   mla_out_norm, conv_out_norm, fox_out_norm, w_out, ffn_norm, w_up,
                   ffn_conv_w, ffn_conv_b, w_down, ple_norm, w_ple_gate, w_ple):
    sizes = (Q_RANK, KV_RANK, MLA_ROPE, CONV_WIDTH, CONV_WIDTH, CONV_WIDTH,
             FOX_HEADS * FOX_DIM, FOX_HEADS * FOX_DIM, FOX_HEADS * FOX_DIM, FOX_HEADS)
    splits = [sum(sizes[:j + 1]) for j in range(len(sizes) - 1)]
    wq, wkv, wr, wb, wc, wh, wfq, wfk, wfv, wff = jnp.split(w_in[i], splits, axis=-1)
    half = MLA_ROPE // 2
    zeros = lambda n: jnp.zeros((D_MODEL, n), F32)
    wr_swapped = jnp.concatenate([wr[:, half:], wr[:, :half]], axis=-1)
    w_in_p = jnp.concatenate([
        wq, wkv,
        zeros(MLA_NOPE), wr, zeros(LANES - MLA_NOPE - MLA_ROPE),
        zeros(MLA_NOPE), wr_swapped, zeros(LANES - MLA_NOPE - MLA_ROPE),
        wb, wc, wh,
        _head_rows(wfq, FOX_DIM, (0,) * FOX_HEADS),
        _head_rows(wfk, FOX_DIM, (0,) * FOX_HEADS),
        wff, zeros(LANES - FOX_HEADS)], axis=-1).astype(BF16)

    uq = w_uq[i].reshape(Q_RANK, MLA_HEADS, MLA_NOPE + MLA_ROPE)
    uq_main = _head_rows(uq.reshape(Q_RANK, -1), MLA_NOPE + MLA_ROPE, (0,) * MLA_HEADS)
    uq_rot = jnp.concatenate([uq[..., MLA_NOPE + half:], uq[..., MLA_NOPE:MLA_NOPE + half]], axis=-1)
    uq_swap = _head_rows(uq_rot.reshape(Q_RANK, -1), MLA_ROPE, (MLA_NOPE,) * MLA_HEADS)
    ukv = w_ukv[i].reshape(KV_RANK, MLA_HEADS, MLA_NOPE + MLA_V)
    uk = _head_rows(ukv[..., :MLA_NOPE].reshape(KV_RANK, -1), MLA_NOPE, (0,) * MLA_HEADS)
    uv = _head_rows(ukv[..., MLA_NOPE:].reshape(KV_RANK, -1), MLA_V, V_OFFSETS)

    row = lambda v: v[i].reshape(1, -1)
    return {
        "attn_norm": row(attn_norm),
        "w_in": w_in_p,
        "q_norm": row(q_norm),
        "w_uq": jnp.concatenate([uq_main, uq_swap], axis=-1).astype(BF16),
        "kv_norm": row(kv_norm),
        "w_uk": uk.astype(BF16),
        "w_uv_t": uv.T.astype(BF16),
        "w_fv_t": _head_rows(wfv, FOX_DIM, V_OFFSETS).T.astype(BF16),
        "v_ones": _v_ones(),
        "conv_w": conv_w[i],
        "conv_out_norm": row(conv_out_norm),
        "b_forget": jnp.pad(b_forget[i], (0, LANES - FOX_HEADS)).reshape(1, LANES),
        "place_q": _placement(L_QC, 1.0),
        "place_k": _placement(L_KC, -1.0),
        "rows": _const_rows(),
        "mla_out_norm": row(mla_out_norm),
        "fox_out_norm": row(fox_out_norm),
        "w_out": w_out[i].astype(BF16),
        "ffn_norm": row(ffn_norm),
        "w_up": w_up[i].astype(BF16),
        "ffn_conv_w": ffn_conv_w[i],
        "ffn_conv_b": row(ffn_conv_b),
        "w_down": w_down[i].astype(BF16),
        "ple_norm": row(ple_norm),
        "w_ple_gate": w_ple_gate[i].astype(BF16),
        "w_ple": w_ple[i].astype(BF16),
    }


def _rope_rows(positions):
    inv_freq = ROPE_THETA ** (-jnp.arange(0, MLA_ROPE, 2, dtype=F32) / MLA_ROPE)
    ang = positions.astype(F32)[..., None] * inv_freq
    cos, sin = jnp.cos(ang), jnp.sin(ang)
    lead = positions.shape + (MLA_NOPE,)
    tail = positions.shape + (LANES - MLA_NOPE - MLA_ROPE,)
    cos_t = jnp.concatenate([jnp.ones(lead, F32), cos, cos, jnp.zeros(tail, F32)], axis=-1)
    sin_t = jnp.concatenate([jnp.zeros(lead, F32), -sin, sin, jnp.zeros(tail, F32)], axis=-1)
    return cos_t, sin_t


def kernel(x, p, positions, attn_norm, w_in, b_forget, q_norm, w_uq, kv_norm, w_ukv, conv_w,
           mla_out_norm, conv_out_norm, fox_out_norm, w_out, ffn_norm, w_up, ffn_conv_w,
           ffn_conv_b, w_down, ple_norm, w_ple_gate, w_ple, final_norm):
    depth = w_in.shape[0]
    cos_t, sin_t = _rope_rows(positions)
    fin = final_norm.reshape(1, D_MODEL)
    h = x
    for i in range(depth):
        w = _layer_weights(i, attn_norm, w_in, b_forget, q_norm, w_uq, kv_norm, w_ukv, conv_w,
                           mla_out_norm, conv_out_norm, fox_out_norm, w_out, ffn_norm, w_up,
                           ffn_conv_w, ffn_conv_b, w_down, ple_norm, w_ple_gate, w_ple)
        q, k, vt, oc = _in_call(h, cos_t, sin_t, w)
        o = _attn_call(q, k, vt)
        h = _out_call(o, oc, h, w)
        h = _ffn_call(h, w)
        h = _ple_call(h, p, i, w, fin, final=(i == depth - 1))
    return h
```

```python
import functools
import math

import jax
import jax.numpy as jnp
from jax import lax
from jax.experimental import pallas as pl
from jax.experimental.pallas import tpu as pltpu

F32 = jnp.float32
BF16 = jnp.bfloat16

D_MODEL = 1024
MLA_HEADS = 6
MLA_NOPE = 64
MLA_ROPE = 32
MLA_V = 64
Q_RANK = 256
KV_RANK = 128
FOX_HEADS = 6
FOX_DIM = 64
CONV_WIDTH = 256
CONV_K = 3
D_FF = 2816
PLE_DIM = 256
ROPE_THETA = 10000.0
EPS = 1e-6
LOG2E = math.log2(math.e)

LANES = 128
BF16_SUBLANES = 16
V7X_VMEM_BYTES = 64 * 1024 * 1024
VMEM_LIMIT = V7X_VMEM_BYTES - 8 * 1024 * 1024

HEADS = MLA_HEADS + FOX_HEADS
HW = MLA_HEADS * LANES
ATT_W = HEADS * 64

C_ZQ = 0
C_ZKV = C_ZQ + Q_RANK
C_ZR = C_ZKV + KV_RANK
C_ZRS = C_ZR + LANES
C_ZB = C_ZRS + LANES
C_ZC = C_ZB + CONV_WIDTH
C_ZH = C_ZC + CONV_WIDTH
C_FQ = C_ZH + CONV_WIDTH
C_FK = C_FQ + HW
C_FF = C_FK + HW
N_Z = C_FF + LANES

L_QC = 64
L_KC = 67

TS_IN = 512
T_ATT = 512
MASK_BIAS = -1e30
TS_OUT = 512
TS_FFN = 512
TF_FFN = 1408
HALO = BF16_SUBLANES
TS_PLE = 512


def _rms(x, g):
    return x * lax.rsqrt(jnp.mean(x * x, axis=-1, keepdims=True) + EPS) * g


def _split3(x):
    hi = x.astype(BF16)
    r = x - hi.astype(F32)
    mid = r.astype(BF16)
    lo = (r - mid.astype(F32)).astype(BF16)
    return hi, mid, lo


def _dot(a, b):
    return jnp.dot(a, b, preferred_element_type=F32)


def _dot_nt(a, b):
    return lax.dot_general(a, b, (((1,), (1,)), ((), ())), preferred_element_type=F32)


def _in_kernel(x_ref, g_ref, win_ref, qn_ref, wuq_ref, kvn_ref, wuk_ref, wuvt_ref, wfvt_ref,
               cos_ref, sin_ref, cw_ref, cn_ref, bf_ref, pq_ref, pk_ref, rows_ref, vones_ref,
               q_ref, k_ref, vt_ref, oc_ref, xs_sc, cum_sc, *, ts):
    @pl.when(pl.program_id(1) == 0)
    def _():
        xs_sc[0:8, :] = jnp.zeros((8, CONV_WIDTH), F32)
        cum_sc[...] = jnp.zeros_like(cum_sc)

    a = _rms(x_ref[0], g_ref[...]).astype(BF16)

    def proj(lo, width):
        return _dot(a, win_ref[:, lo:lo + width])

    cos_t = cos_ref[0]
    sin_t = sin_ref[0]

    qn = _rms(proj(C_ZQ, Q_RANK), qn_ref[...]).astype(BF16)
    q_main = _dot(qn, wuq_ref[:, 0:HW])
    q_swap = _dot(qn, wuq_ref[:, HW:2 * HW])
    mla_scale = LOG2E * (MLA_NOPE + MLA_ROPE) ** -0.5
    kvn = _rms(proj(C_ZKV, KV_RANK), kvn_ref[...]).astype(BF16)
    k_nope = _dot(kvn, wuk_ref[...])
    v_ones = jnp.tile(vones_ref[...], (1, ts // LANES))
    v_mla_t = _dot_nt(wuvt_ref[...], kvn) + v_ones
    k_rope = proj(C_ZR, LANES) * cos_t + proj(C_ZRS, LANES) * sin_t
    for h in range(MLA_HEADS):
        sl = slice(h * LANES, (h + 1) * LANES)
        q_ref[0, h] = ((q_main[:, sl] * cos_t + q_swap[:, sl] * sin_t) * mla_scale).astype(BF16)
        k_ref[0, h] = (k_nope[:, sl] + k_rope).astype(BF16)
        vt_ref[0, h, 0] = v_mla_t[sl, :].astype(BF16)

    xc = proj(C_ZC, CONV_WIDTH) * proj(C_ZH, CONV_WIDTH)
    xs_sc[8:8 + ts, :] = xc
    y = (cw_ref[0:1, :] * xs_sc[6:6 + ts, :] + cw_ref[1:2, :] * xs_sc[7:7 + ts, :]
         + cw_ref[2:3, :] * xc)
    oc_ref[0] = _rms(proj(C_ZB, CONV_WIDTH) * y, cn_ref[...]).astype(BF16)
    xs_sc[0:8, :] = xs_sc[ts:ts + 8, :]

    ff = proj(C_FF, LANES) + bf_ref[...]
    log_f = jnp.minimum(ff, 0.0) - jnp.log1p(jnp.exp(-jnp.abs(ff)))
    row = lax.broadcasted_iota(jnp.int32, (ts, ts), 0)
    col = lax.broadcasted_iota(jnp.int32, (ts, ts), 1)
    tri = (col <= row).astype(BF16)
    f_hi, f_mid, f_lo = _split3(log_f)
    cum = _dot(tri, f_hi) + _dot(tri, f_mid) + _dot(tri, f_lo) + cum_sc[0:1, :]
    cum_sc[0:1, :] = cum[ts - 1:ts, :]
    c_hi, c_mid, c_lo = _split3(cum * LOG2E)
    aug_q = _dot(c_hi, pq_ref[0]) + _dot(c_mid, pq_ref[1]) + _dot(c_lo, pq_ref[2])
    aug_k = _dot(c_hi, pk_ref[0]) + _dot(c_mid, pk_ref[1]) + _dot(c_lo, pk_ref[2])

    fq = proj(C_FQ, HW)
    fk = proj(C_FK, HW)
    fv_t = _dot_nt(wfvt_ref[...], a) + v_ones
    fox_scale = LOG2E * FOX_DIM ** -0.5
    for h in range(FOX_HEADS):
        sl = slice(h * LANES, (h + 1) * LANES)
        q_ref[0, MLA_HEADS + h] = (fq[:, sl] * fox_scale + aug_q[:, sl] + rows_ref[0:1, sl]).astype(BF16)
        k_ref[0, MLA_HEADS + h] = (fk[:, sl] + aug_k[:, sl] + rows_ref[1:2, sl]).astype(BF16)
        vt_ref[0, MLA_HEADS + h, 0] = fv_t[sl, :].astype(BF16)


def _in_call(h, cos_t, sin_t, w, *, ts=TS_IN):
    b, s, _ = h.shape
    const = lambda shape: pl.BlockSpec(shape, lambda bi, si: (0,) * len(shape))
    assert ts == T_ATT
    head_spec = pl.BlockSpec((1, HEADS, ts, LANES), lambda bi, si: (bi, 0, si, 0))
    head_shape = jax.ShapeDtypeStruct((b, HEADS, s, LANES), BF16)
    vt_spec = pl.BlockSpec((1, HEADS, 1, LANES, ts), lambda bi, si: (bi, 0, si, 0, 0))
    vt_shape = jax.ShapeDtypeStruct((b, HEADS, s // ts, LANES, ts), BF16)
    return pl.pallas_call(
        functools.partial(_in_kernel, ts=ts),
        grid=(b, s // ts),
        in_specs=[
            pl.BlockSpec((1, ts, D_MODEL), lambda bi, si: (bi, si, 0)),
            const((1, D_MODEL)),
            const((D_MODEL, N_Z)),
            const((1, Q_RANK)),
            const((Q_RANK, 2 * HW)),
            const((1, KV_RANK)),
            const((KV_RANK, HW)),
            const((HW, KV_RANK)),
            const((HW, D_MODEL)),
            pl.BlockSpec((1, ts, LANES), lambda bi, si: (bi, si, 0)),
            pl.BlockSpec((1, ts, LANES), lambda bi, si: (bi, si, 0)),
            const((CONV_K, CONV_WIDTH)),
            const((1, CONV_WIDTH)),
            const((1, LANES)),
            const((3, LANES, HW)),
            const((3, LANES, HW)),
            const((2, HW)),
            const((HW, LANES)),
        ],
        out_specs=[head_spec, head_spec, vt_spec,
                   pl.BlockSpec((1, ts, CONV_WIDTH), lambda bi, si: (bi, si, 0))],
        out_shape=[head_shape, head_shape, vt_shape,
                   jax.ShapeDtypeStruct((b, s, CONV_WIDTH), BF16)],
        scratch_shapes=[pltpu.VMEM((ts + 8, CONV_WIDTH), F32), pltpu.VMEM((8, LANES), F32)],
        compiler_params=pltpu.CompilerParams(
            dimension_semantics=("arbitrary", "arbitrary"), vmem_limit_bytes=VMEM_LIMIT),
        name="in_proj",
    )(h, w["attn_norm"], w["w_in"], w["q_norm"], w["w_uq"], w["kv_norm"], w["w_uk"], w["w_uv_t"],
      w["w_fv_t"], cos_t, sin_t, w["conv_w"], w["conv_out_norm"], w["b_forget"], w["place_q"],
      w["place_k"], w["rows"], w["v_ones"])


def _attn_kernel(q_ref, k_ref, vt_ref, bias_ref, o_ref, s_sc, m_sc, acc_sc, *, t, nb):
    m_sc[...] = jnp.full_like(m_sc, -jnp.inf)
    acc_sc[...] = jnp.zeros_like(acc_sc)
    row = lax.broadcasted_iota(jnp.int32, (LANES, t), 0)

    def chunk(ref, hh, j):
        return ref[0, hh, pl.ds(pl.multiple_of(j * t, t), t), :]

    def scores(hh, i, j, slot, diag):
        s_t = _dot_nt(chunk(k_ref, hh, j), chunk(q_ref, hh, i))
        s_sc[slot, hh] = s_t + bias_ref[...] if diag else s_t

    def consume(hh, i, j, slot):
        m_prev = m_sc[hh, i]
        m_new = jnp.maximum(m_prev, jnp.max(s_sc[slot, hh], axis=0, keepdims=True))
        alpha = jnp.exp2(m_prev - m_new)
        p_t = jnp.exp2(s_sc[slot, hh] - m_new).astype(BF16)
        acc = alpha * acc_sc[hh, i] + _dot(vt_ref[0, hh, j], p_t)
        acc_sc[hh, i] = acc
        m_sc[hh, i] = m_new
        return acc

    def below(i, j, slot):
        wrap = j + 1 == i
        ni = jnp.where(wrap, jnp.minimum(i + 1, nb - 1), i)
        nj = jnp.where(wrap, 0, j + 1)
        for hh in range(2):
            scores(hh, ni, nj, 1 - slot, False)
        for hh in range(2):
            consume(hh, i, j, slot)
        return ni, nj

    first = (jnp.int32(1), jnp.int32(0))
    for hh in range(2):
        scores(hh, *first, 0, False)
    n_below = nb * (nb - 1) // 2
    assert n_below % 2 == 0 and nb % 2 == 0
    lax.fori_loop(0, n_below // 2, lambda _, unit: below(*below(*unit, 0), 1), first)

    def diagonal(i, slot):
        ni = jnp.minimum(i + 1, nb - 1)
        for hh in range(2):
            scores(hh, ni, ni, 1 - slot, True)
        acc0 = consume(0, i, i, slot)
        acc1 = consume(1, i, i, slot)
        o_t = jnp.where(row < 64, acc0 / acc0[64:65, :], acc1 / acc1[0:1, :])
        o_ref[0, pl.ds(pl.multiple_of(i * t, t), t), :] = o_t.T
        return ni

    zero = jnp.int32(0)
    for hh in range(2):
        scores(hh, zero, zero, 0, True)
    lax.fori_loop(0, nb // 2, lambda _, i: diagonal(diagonal(i, 0), 1), zero)


def _attn_call(q, k, vt, *, t=T_ATT):
    b, _, s, _ = q.shape
    nb = s // t
    key = lax.broadcasted_iota(jnp.int32, (t, t), 0)
    query = lax.broadcasted_iota(jnp.int32, (t, t), 1)
    bias = jnp.where(key <= query, 0.0, MASK_BIAS).astype(F32)
    seq = pl.BlockSpec((1, 2, s, LANES), lambda bi, g: (bi, g, 0, 0))
    return pl.pallas_call(
        functools.partial(_attn_kernel, t=t, nb=nb),
        grid=(b, HEADS // 2),
        in_specs=[seq, seq,
                  pl.BlockSpec((1, 2, nb, LANES, t), lambda bi, g: (bi, g, 0, 0, 0)),
                  pl.BlockSpec((t, t), lambda bi, g: (0, 0))],
        out_specs=pl.BlockSpec((1, s, LANES), lambda bi, g: (bi, 0, g)),
        out_shape=jax.ShapeDtypeStruct((b, s, ATT_W), F32),
        scratch_shapes=[pltpu.VMEM((2, 2, t, t), F32), pltpu.VMEM((2, nb, 1, t), F32),
                        pltpu.VMEM((2, nb, LANES, t), F32)],
        compiler_params=pltpu.CompilerParams(
            dimension_semantics=("arbitrary", "arbitrary"), vmem_limit_bytes=VMEM_LIMIT),
        name="attention",
    )(q, k, vt, bias)


def _out_kernel(o_ref, oc_ref, h_ref, gm_ref, gf_ref, wout_ref, out_ref):
    o = o_ref[0]
    half = ATT_W // 2
    mixed = jnp.concatenate([
        _rms(o[:, :half], gm_ref[...]).astype(BF16),
        oc_ref[0],
        _rms(o[:, half:], gf_ref[...]).astype(BF16)], axis=-1)
    out_ref[0] = h_ref[0] + _dot(mixed, wout_ref[...])


def _out_call(o, oc, h, w, *, ts=TS_OUT):
    b, s, _ = h.shape
    const = lambda shape: pl.BlockSpec(shape, lambda bi, si: (0,) * len(shape))
    row = lambda width: pl.BlockSpec((1, ts, width), lambda bi, si: (bi, si, 0))
    return pl.pallas_call(
        _out_kernel,
        grid=(b, s // ts),
        in_specs=[row(ATT_W), row(CONV_WIDTH), row(D_MODEL),
                  const((1, ATT_W // 2)), const((1, ATT_W // 2)), const((D_MODEL, D_MODEL))],
        out_specs=row(D_MODEL),
        out_shape=jax.ShapeDtypeStruct(h.shape, F32),
        compiler_params=pltpu.CompilerParams(
            dimension_semantics=("arbitrary", "arbitrary"), vmem_limit_bytes=VMEM_LIMIT),
        name="out_proj",
    )(o, oc, h, w["mla_out_norm"], w["fox_out_norm"], w["w_out"])


def _ffn_kernel(hp_ref, h_ref, g_ref, wg_ref, wv_ref, cwg_ref, cwv_ref, cbg_ref, cbv_ref, wd_ref,
                out_ref, m_sc, *, ts):
    c = pl.program_id(2)

    @pl.when(c == 0)
    def _():
        keep = (pl.program_id(1) > 0).astype(F32)
        m_sc[0:HALO, :] = _rms(hp_ref[0] * keep, g_ref[...]).astype(BF16)
        m_sc[HALO:HALO + ts, :] = _rms(h_ref[0], g_ref[...]).astype(BF16)

    m = m_sc[...]

    def conv(u, cw_ref, cb_ref):
        return (cw_ref[0:1, :] * u[HALO - 2:HALO - 2 + ts, :]
                + cw_ref[1:2, :] * u[HALO - 1:HALO - 1 + ts, :]
                + cw_ref[2:3, :] * u[HALO:HALO + ts, :] + cb_ref[...])

    gate = conv(_dot(m, wg_ref[...]), cwg_ref, cbg_ref)
    val = conv(_dot(m, wv_ref[...]), cwv_ref, cbv_ref)
    act = (gate * jax.nn.sigmoid(gate) * val).astype(BF16)
    part = _dot(act, wd_ref[...])

    @pl.when(c == 0)
    def _():
        out_ref[0] = h_ref[0] + part

    @pl.when(c > 0)
    def _():
        out_ref[0] += part


def _ffn_call(h, w, *, ts=TS_FFN, tf=TF_FFN):
    b, s, _ = h.shape
    nf = D_FF // tf
    halo_blocks = ts // HALO
    return pl.pallas_call(
        functools.partial(_ffn_kernel, ts=ts),
        grid=(b, s // ts, nf),
        in_specs=[
            pl.BlockSpec((1, HALO, D_MODEL),
                         lambda bi, si, c: (bi, jnp.maximum(si * halo_blocks - 1, 0), 0)),
            pl.BlockSpec((1, ts, D_MODEL), lambda bi, si, c: (bi, si, 0)),
            pl.BlockSpec((1, D_MODEL), lambda bi, si, c: (0, 0)),
            pl.BlockSpec((D_MODEL, tf), lambda bi, si, c: (0, c)),
            pl.BlockSpec((D_MODEL, tf), lambda bi, si, c: (0, nf + c)),
            pl.BlockSpec((CONV_K, tf), lambda bi, si, c: (0, c)),
            pl.BlockSpec((CONV_K, tf), lambda bi, si, c: (0, nf + c)),
            pl.BlockSpec((1, tf), lambda bi, si, c: (0, c)),
            pl.BlockSpec((1, tf), lambda bi, si, c: (0, nf + c)),
            pl.BlockSpec((tf, D_MODEL), lambda bi, si, c: (c, 0)),
        ],
        out_specs=pl.BlockSpec((1, ts, D_MODEL), lambda bi, si, c: (bi, si, 0)),
        out_shape=jax.ShapeDtypeStruct(h.shape, F32),
        scratch_shapes=[pltpu.VMEM((ts + HALO, D_MODEL), BF16)],
        compiler_params=pltpu.CompilerParams(
            dimension_semantics=("arbitrary", "arbitrary", "arbitrary"),
            vmem_limit_bytes=VMEM_LIMIT),
        name="ffn",
    )(h, h, w["ffn_norm"], w["w_up"], w["w_up"], w["ffn_conv_w"], w["ffn_conv_w"],
      w["ffn_conv_b"], w["ffn_conv_b"], w["w_down"])


def _ple_kernel(h_ref, p_ref, g_ref, wpg_ref, wple_ref, fg_ref, out_ref, *, final):
    h = h_ref[0]
    gate = jax.nn.sigmoid(_dot(_rms(h, g_ref[...]).astype(BF16), wpg_ref[...]))
    out = h + gate * _dot(p_ref[0, 0].astype(BF16), wple_ref[...])
    if final:
        out = _rms(out, fg_ref[...])
    out_ref[0] = out


def _ple_call(h, p, layer, w, final_norm, *, final, ts=TS_PLE):
    b, s, _ = h.shape
    const = lambda shape: pl.BlockSpec(shape, lambda bi, si: (0,) * len(shape))
    row = pl.BlockSpec((1, ts, D_MODEL), lambda bi, si: (bi, si, 0))
    return pl.pallas_call(
        functools.partial(_ple_kernel, final=final),
        grid=(b, s // ts),
        in_specs=[row,
                  pl.BlockSpec((1, 1, ts, PLE_DIM), lambda bi, si: (layer, bi, si, 0)),
                  const((1, D_MODEL)), const((D_MODEL, D_MODEL)), const((PLE_DIM, D_MODEL)),
                  const((1, D_MODEL))],
        out_specs=row,
        out_shape=jax.ShapeDtypeStruct(h.shape, F32),
        compiler_params=pltpu.CompilerParams(
            dimension_semantics=("arbitrary", "arbitrary"), vmem_limit_bytes=VMEM_LIMIT),
        name="ple",
    )(h, p, w["ple_norm"], w["w_ple_gate"], w["w_ple"], final_norm)


def _head_rows(w, width, offsets):
    kdim = w.shape[0]
    w = w.reshape(kdim, MLA_HEADS, width)
    out = jnp.zeros((kdim, MLA_HEADS, LANES), w.dtype)
    for h, off in enumerate(offsets):
        out = out.at[:, h, off:off + width].set(w[:, h])
    return out.reshape(kdim, HW)


V_OFFSETS = tuple(0 if h % 2 == 0 else 64 for h in range(MLA_HEADS))


def _placement(lane0, sign):
    mats = []
    for t in range(3):
        m = jnp.zeros((LANES, MLA_HEADS, LANES), F32)
        for h in range(FOX_HEADS):
            m = m.at[h, h, lane0 + t].set(sign)
        mats.append(m.reshape(LANES, HW))
    return jnp.stack(mats).astype(BF16)


def _const_rows():
    rows = jnp.zeros((2, MLA_HEADS, LANES), F32)
    rows = rows.at[0, :, L_KC:L_KC + 3].set(1.0)
    rows = rows.at[1, :, L_QC:L_QC + 3].set(1.0)
    return rows.reshape(2, HW)


def _v_ones():
    col = jnp.zeros((MLA_HEADS, LANES), F32)
    for h in range(MLA_HEADS):
        col = col.at[h, 64 if h % 2 == 0 else 0].set(1.0)
    return jnp.broadcast_to(col.reshape(HW, 1), (HW, LANES))


def _layer_weights(i, attn_norm, w_in, b_forget, q_norm, w_uq, kv_norm, w_ukv, conv_w,
                   mla_out_norm, conv_out_norm, fox_out_norm, w_out, ffn_norm, w_up,
                   ffn_conv_w, ffn_conv_b, w_down, ple_norm, w_ple_gate, w_ple):
    sizes = (Q_RANK, KV_RANK, MLA_ROPE, CONV_WIDTH, CONV_WIDTH, CONV_WIDTH,
             FOX_HEADS * FOX_DIM, FOX_HEADS * FOX_DIM, FOX_HEADS * FOX_DIM, FOX_HEADS)
    splits = [sum(sizes[:j + 1]) for j in range(len(sizes) - 1)]
    wq, wkv, wr, wb, wc, wh, wfq, wfk, wfv, wff = jnp.split(w_in[i], splits, axis=-1)
    half = MLA_ROPE // 2
    zeros = lambda n: jnp.zeros((D_MODEL, n), F32)
    wr_swapped = jnp.concatenate([wr[:, half:], wr[:, :half]], axis=-1)
    w_in_p = jnp.concatenate([
        wq, wkv,
        zeros(MLA_NOPE), wr, zeros(LANES - MLA_NOPE - MLA_ROPE),
        zeros(MLA_NOPE), wr_swapped, zeros(LANES - MLA_NOPE - MLA_ROPE),
        wb, wc, wh,
        _head_rows(wfq, FOX_DIM, (0,) * FOX_HEADS),
        _head_rows(wfk, FOX_DIM, (0,) * FOX_HEADS),
        wff, zeros(LANES - FOX_HEADS)], axis=-1).astype(BF16)

    uq = w_uq[i].reshape(Q_RANK, MLA_HEADS, MLA_NOPE + MLA_ROPE)
    uq_main = _head_rows(uq.reshape(Q_RANK, -1), MLA_NOPE + MLA_ROPE, (0,) * MLA_HEADS)
    uq_rot = jnp.concatenate([uq[..., MLA_NOPE + half:], uq[..., MLA_NOPE:MLA_NOPE + half]], axis=-1)
    uq_swap = _head_rows(uq_rot.reshape(Q_RANK, -1), MLA_ROPE, (MLA_NOPE,) * MLA_HEADS)
    ukv = w_ukv[i].reshape(KV_RANK, MLA_HEADS, MLA_NOPE + MLA_V)
    uk = _head_rows(ukv[..., :MLA_NOPE].reshape(KV_RANK, -1), MLA_NOPE, (0,) * MLA_HEADS)
    uv = _head_rows(ukv[..., MLA_NOPE:].reshape(KV_RANK, -1), MLA_V, V_OFFSETS)

    row = lambda v: v[i].reshape(1, -1)
    return {
        "attn_norm": row(attn_norm),
        "w_in": w_in_p,
        "q_norm": row(q_norm),
        "w_uq": jnp.concatenate([uq_main, uq_swap], axis=-1).astype(BF16),
        "kv_norm": row(kv_norm),
        "w_uk": uk.astype(BF16),
        "w_uv_t": uv.T.astype(BF16),
        "w_fv_t": _head_rows(wfv, FOX_DIM, V_OFFSETS).T.astype(BF16),
        "v_ones": _v_ones(),
        "conv_w": conv_w[i],
        "conv_out_norm": row(conv_out_norm),
        "b_forget": jnp.pad(b_forget[i], (0, LANES - FOX_HEADS)).reshape(1, LANES),
        "place_q": _placement(L_QC, 1.0),
        "place_k": _placement(L_KC, -1.0),
        "rows": _const_rows(),
        "mla_out_norm": row(mla_out_norm),
        "fox_out_norm": row(fox_out_norm),
        "w_out": w_out[i].astype(BF16),
        "ffn_norm": row(ffn_norm),
        "w_up": w_up[i].astype(BF16),
        "ffn_conv_w": ffn_conv_w[i],
        "ffn_conv_b": row(ffn_conv_b),
        "w_down": w_down[i].astype(BF16),
        "ple_norm": row(ple_norm),
        "w_ple_gate": w_ple_gate[i].astype(BF16),
        "w_ple": w_ple[i].astype(BF16),
    }


def _rope_rows(positions):
    inv_freq = ROPE_THETA ** (-jnp.arange(0, MLA_ROPE, 2, dtype=F32) / MLA_ROPE)
    ang = positions.astype(F32)[..., None] * inv_freq
    cos, sin = jnp.cos(ang), jnp.sin(ang)
    lead = positions.shape + (MLA_NOPE,)
    tail = positions.shape + (LANES - MLA_NOPE - MLA_ROPE,)
    cos_t = jnp.concatenate([jnp.ones(lead, F32), cos, cos, jnp.zeros(tail, F32)], axis=-1)
    sin_t = jnp.concatenate([jnp.zeros(lead, F32), -sin, sin, jnp.zeros(tail, F32)], axis=-1)
    return cos_t, sin_t


def kernel(x, p, positions, attn_norm, w_in, b_forget, q_norm, w_uq, kv_norm, w_ukv, conv_w,
           mla_out_norm, conv_out_norm, fox_out_norm, w_out, ffn_norm, w_up, ffn_conv_w,
           ffn_conv_b, w_down, ple_norm, w_ple_gate, w_ple, final_norm):
    depth = w_in.shape[0]
    cos_t, sin_t = _rope_rows(positions)
    fin = final_norm.reshape(1, D_MODEL)
    h = x
    for i in range(depth):
        w = _layer_weights(i, attn_norm, w_in, b_forget, q_norm, w_uq, kv_norm, w_ukv, conv_w,
                           mla_out_norm, conv_out_norm, fox_out_norm, w_out, ffn_norm, w_up,
                           ffn_conv_w, ffn_conv_b, w_down, ple_norm, w_ple_gate, w_ple)
        q, k, vt, oc = _in_call(h, cos_t, sin_t, w)
        o = _attn_call(q, k, vt)
        h = _out_call(o, oc, h, w)
        h = _ffn_call(h, w)
        h = _ple_call(h, p, i, w, fin, final=(i == depth - 1))
    return h
```

```python
import functools
import math

import jax
import jax.numpy as jnp
import numpy as np
from jax import lax
from jax.experimental import pallas as pl
from jax.experimental.pallas import tpu as pltpu

F32 = jnp.float32
BF16 = jnp.bfloat16

D_MODEL = 1024
MLA_HEADS = 6
MLA_NOPE = 64
MLA_ROPE = 32
MLA_V = 64
Q_RANK = 256
KV_RANK = 128
FOX_HEADS = 6
FOX_DIM = 64
FOX_W = FOX_HEADS * FOX_DIM
CONV_WIDTH = 256
CONV_K = 3
D_FF = 2816
PLE_DIM = 256
ROPE_THETA = 10000.0
EPS = 1e-6
LOG2E = math.log2(math.e)

IN_SIZES = (Q_RANK, KV_RANK, MLA_ROPE, CONV_WIDTH, CONV_WIDTH, CONV_WIDTH, FOX_W, FOX_W, FOX_W,
            FOX_HEADS)
O_ZQ, O_ZKV, O_ZR, O_ZB, O_ZC, O_ZH, O_FQ, O_FK, O_FV, O_FF, O_END = (
    sum(IN_SIZES[:n]) for n in range(len(IN_SIZES) + 1))

LANES = 128
HALF = LANES // 2
BF16_SUBLANES = 16
V7X_VMEM_BYTES = 64 * 1024 * 1024
VMEM_LIMIT = V7X_VMEM_BYTES - 8 * 1024 * 1024

HEADS = MLA_HEADS + FOX_HEADS
HW = MLA_HEADS * LANES
ATT_W = HEADS * 64

C_ZQ = 0
C_ZKV = C_ZQ + Q_RANK
C_ZR = C_ZKV + KV_RANK
C_ZRS = C_ZR + LANES
C_ZB = C_ZRS + LANES
C_ZC = C_ZB + CONV_WIDTH
C_ZH = C_ZC + CONV_WIDTH
C_FQ = C_ZH + CONV_WIDTH
C_FK = C_FQ + FOX_W
C_FF = C_FK + FOX_W
N_Z = C_FF + LANES

TS_IN = 512
T_ATT = 512
MASK_BIAS = -1e30
TS_OUT = 512
TS_FFN = 512
TF_FFN = 1408
HALO = BF16_SUBLANES
TS_PLE = 512


def _rms(x, g):
    return x * lax.rsqrt(jnp.mean(x * x, axis=-1, keepdims=True) + EPS) * g


def _split3(x):
    hi = x.astype(BF16)
    r = x - hi.astype(F32)
    mid = r.astype(BF16)
    lo = (r - mid.astype(F32)).astype(BF16)
    return hi, mid, lo


def _dot(a, b):
    return jnp.dot(a, b, preferred_element_type=F32)


def _dot_nt(a, b):
    return lax.dot_general(a, b, (((1,), (1,)), ((), ())), preferred_element_type=F32)


def _layer_spec(layer, *shape):
    return pl.BlockSpec((None,) + shape, lambda *_: (layer,) + (0,) * len(shape))


def _const_spec(*shape):
    return pl.BlockSpec(shape, lambda *_: (0,) * len(shape))


def _in_kernel(x_ref, g_ref, win_ref, qn_ref, wuq_ref, kvn_ref, wuk_ref, wuvt_ref, wfvt_ref,
               cos_ref, sin_ref, cw_ref, cn_ref, bf_ref, pq_ref, pk_ref, rows_ref, vones_ref,
               q_ref, k_ref, vt_ref, oc_ref, xs_sc, cum_sc, *, ts):
    @pl.when(pl.program_id(1) == 0)
    def _():
        xs_sc[0:8, :] = jnp.zeros((8, CONV_WIDTH), F32)
        cum_sc[...] = jnp.zeros_like(cum_sc)

    a = _rms(x_ref[0], g_ref[...]).astype(BF16)

    def proj(lo, width):
        return _dot(a, win_ref[:, lo:lo + width])

    cos_t = cos_ref[0]
    sin_t = sin_ref[0]
    low_lanes = lax.broadcasted_iota(jnp.int32, (ts, LANES), 1) < HALF
    top_rows = lax.broadcasted_iota(jnp.int32, (LANES, ts), 0) < HALF

    def head_blocks(h):
        return (slice(h * LANES, (h + 1) * LANES),
                slice((h // 2) * LANES, (h // 2 + 1) * LANES), h % 2 == 0)

    def v_row(pair_t, h):
        own, pair, even = head_blocks(h)
        mine = top_rows if even else jnp.logical_not(top_rows)
        ones = jnp.tile(vones_ref[own, :], (1, ts // LANES))
        return jnp.where(mine, pair_t[pair, :], ones).astype(BF16)

    qn = _rms(proj(C_ZQ, Q_RANK), qn_ref[...]).astype(BF16)
    q_main = _dot(qn, wuq_ref[:, 0:HW])
    q_swap = _dot(qn, wuq_ref[:, HW:2 * HW])
    mla_scale = LOG2E * (MLA_NOPE + MLA_ROPE) ** -0.5
    kvn = _rms(proj(C_ZKV, KV_RANK), kvn_ref[...]).astype(BF16)
    k_nope = _dot(kvn, wuk_ref[...])
    v_mla_t = _dot_nt(wuvt_ref[...], kvn)
    k_rope = proj(C_ZR, LANES) * cos_t + proj(C_ZRS, LANES) * sin_t
    for h in range(MLA_HEADS):
        own = head_blocks(h)[0]
        q_ref[0, h] = ((q_main[:, own] * cos_t + q_swap[:, own] * sin_t) * mla_scale).astype(BF16)
        k_ref[0, h] = (k_nope[:, own] + k_rope).astype(BF16)
        vt_ref[0, h, 0] = v_row(v_mla_t, h)

    xc = proj(C_ZC, CONV_WIDTH) * proj(C_ZH, CONV_WIDTH)
    xs_sc[8:8 + ts, :] = xc
    y = (cw_ref[0:1, :] * xs_sc[6:6 + ts, :] + cw_ref[1:2, :] * xs_sc[7:7 + ts, :]
         + cw_ref[2:3, :] * xc)
    oc_ref[0] = _rms(proj(C_ZB, CONV_WIDTH) * y, cn_ref[...]).astype(BF16)
    xs_sc[0:8, :] = xs_sc[ts:ts + 8, :]

    ff = proj(C_FF, LANES) + bf_ref[...]
    log_f = jnp.minimum(ff, 0.0) - jnp.log1p(jnp.exp(-jnp.abs(ff)))
    row = lax.broadcasted_iota(jnp.int32, (ts, ts), 0)
    col = lax.broadcasted_iota(jnp.int32, (ts, ts), 1)
    tri = (col <= row).astype(BF16)
    f_hi, f_mid, f_lo = _split3(log_f)
    cum = _dot(tri, f_hi) + _dot(tri, f_mid) + _dot(tri, f_lo) + cum_sc[0:1, :]
    cum_sc[0:1, :] = cum[ts - 1:ts, :]
    c_hi, c_mid, c_lo = _split3(cum * LOG2E)
    extra_q = (_dot(c_hi, pq_ref[0]) + _dot(c_mid, pq_ref[1]) + _dot(c_lo, pq_ref[2])
               + rows_ref[0:1, :])
    extra_k = (_dot(c_hi, pk_ref[0]) + _dot(c_mid, pk_ref[1]) + _dot(c_lo, pk_ref[2])
               + rows_ref[1:2, :])

    fq = proj(C_FQ, FOX_W) * (LOG2E * FOX_DIM ** -0.5)
    fk = proj(C_FK, FOX_W)
    fv_t = _dot_nt(wfvt_ref[...], a)
    for h in range(FOX_HEADS):
        own, pair, even = head_blocks(h)
        mine = low_lanes if even else jnp.logical_not(low_lanes)
        q_ref[0, MLA_HEADS + h] = jnp.where(mine, fq[:, pair], extra_q[:, own]).astype(BF16)
        k_ref[0, MLA_HEADS + h] = jnp.where(mine, fk[:, pair], extra_k[:, own]).astype(BF16)
        vt_ref[0, MLA_HEADS + h, 0] = v_row(fv_t, h)


def _in_call(h, cos_t, sin_t, layer, w, *, ts=TS_IN):
    b, s, _ = h.shape
    assert ts == T_ATT
    tile = lambda width: pl.BlockSpec((1, ts, width), lambda bi, si: (bi, si, 0))
    head_spec = pl.BlockSpec((1, HEADS, ts, LANES), lambda bi, si: (bi, 0, si, 0))
    head_shape = jax.ShapeDtypeStruct((b, HEADS, s, LANES), BF16)
    vt_spec = pl.BlockSpec((1, HEADS, 1, LANES, ts), lambda bi, si: (bi, 0, si, 0, 0))
    vt_shape = jax.ShapeDtypeStruct((b, HEADS, s // ts, LANES, ts), BF16)
    return pl.pallas_call(
        functools.partial(_in_kernel, ts=ts),
        grid=(b, s // ts),
        in_specs=[
            tile(D_MODEL),
            _layer_spec(layer, 1, D_MODEL),
            _layer_spec(layer, D_MODEL, N_Z),
            _layer_spec(layer, 1, Q_RANK),
            _layer_spec(layer, Q_RANK, 2 * HW),
            _layer_spec(layer, 1, KV_RANK),
            _layer_spec(layer, KV_RANK, HW),
            _layer_spec(layer, MLA_HEADS * MLA_V, KV_RANK),
            _layer_spec(layer, FOX_W, D_MODEL),
            tile(LANES),
            tile(LANES),
            _layer_spec(layer, CONV_K, CONV_WIDTH),
            _layer_spec(layer, 1, CONV_WIDTH),
            _layer_spec(layer, 1, LANES),
            _const_spec(3, LANES, HW),
            _const_spec(3, LANES, HW),
            _const_spec(2, HW),
            _const_spec(HW, LANES),
        ],
        out_specs=[head_spec, head_spec, vt_spec, tile(CONV_WIDTH)],
        out_shape=[head_shape, head_shape, vt_shape,
                   jax.ShapeDtypeStruct((b, s, CONV_WIDTH), BF16)],
        scratch_shapes=[pltpu.VMEM((ts + 8, CONV_WIDTH), F32), pltpu.VMEM((8, LANES), F32)],
        compiler_params=pltpu.CompilerParams(
            dimension_semantics=("arbitrary", "arbitrary"), vmem_limit_bytes=VMEM_LIMIT),
        name="in_proj",
    )(h, w["attn_norm"], w["w_in"], w["q_norm"], w["w_uq"], w["kv_norm"], w["w_uk"], w["w_uv_t"],
      w["w_fv_t"], cos_t, sin_t, w["conv_w"], w["conv_out_norm"], w["b_forget"], w["place_q"],
      w["place_k"], w["extra_ones"], w["v_ones"])


def _attn_kernel(q_ref, k_ref, vt_ref, bias_ref, o_ref, s_sc, m_sc, acc_sc, *, t, nb):
    m_sc[...] = jnp.full_like(m_sc, -jnp.inf)
    acc_sc[...] = jnp.zeros_like(acc_sc)
    row = lax.broadcasted_iota(jnp.int32, (LANES, t), 0)

    def chunk(ref, hh, j):
        return ref[0, hh, pl.ds(pl.multiple_of(j * t, t), t), :]

    def scores(hh, i, j, slot, diag):
        s_t = _dot_nt(chunk(k_ref, hh, j), chunk(q_ref, hh, i))
        s_sc[slot, hh] = s_t + bias_ref[...] if diag else s_t

    def consume(hh, i, j, slot):
        m_prev = m_sc[hh, i]
        m_new = jnp.maximum(m_prev, jnp.max(s_sc[slot, hh], axis=0, keepdims=True))
        alpha = jnp.exp2(m_prev - m_new)
        p_t = jnp.exp2(s_sc[slot, hh] - m_new).astype(BF16)
        acc = alpha * acc_sc[hh, i] + _dot(vt_ref[0, hh, j], p_t)
        acc_sc[hh, i] = acc
        m_sc[hh, i] = m_new
        return acc

    def below(i, j, slot):
        wrap = j + 1 == i
        ni = jnp.where(wrap, jnp.minimum(i + 1, nb - 1), i)
        nj = jnp.where(wrap, 0, j + 1)
        for hh in range(2):
            scores(hh, ni, nj, 1 - slot, False)
        for hh in range(2):
            consume(hh, i, j, slot)
        return ni, nj

    first = (jnp.int32(1), jnp.int32(0))
    for hh in range(2):
        scores(hh, *first, 0, False)
    n_below = nb * (nb - 1) // 2
    assert n_below % 2 == 0 and nb % 2 == 0
    lax.fori_loop(0, n_below // 2, lambda _, unit: below(*below(*unit, 0), 1), first)

    def diagonal(i, slot):
        ni = jnp.minimum(i + 1, nb - 1)
        for hh in range(2):
            scores(hh, ni, ni, 1 - slot, True)
        acc0 = consume(0, i, i, slot)
        acc1 = consume(1, i, i, slot)
        o_t = jnp.where(row < HALF, acc0 / acc0[HALF:HALF + 1, :], acc1 / acc1[0:1, :])
        o_ref[0, pl.ds(pl.multiple_of(i * t, t), t), :] = o_t.T
        return ni

    zero = jnp.int32(0)
    for hh in range(2):
        scores(hh, zero, zero, 0, True)
    lax.fori_loop(0, nb // 2, lambda _, i: diagonal(diagonal(i, 0), 1), zero)


def _attn_call(q, k, vt, *, t=T_ATT):
    b, _, s, _ = q.shape
    nb = s // t
    key = lax.broadcasted_iota(jnp.int32, (t, t), 0)
    query = lax.broadcasted_iota(jnp.int32, (t, t), 1)
    bias = jnp.where(key <= query, 0.0, MASK_BIAS).astype(F32)
    seq = pl.BlockSpec((1, 2, s, LANES), lambda bi, g: (bi, g, 0, 0))
    return pl.pallas_call(
        functools.partial(_attn_kernel, t=t, nb=nb),
        grid=(b, HEADS // 2),
        in_specs=[seq, seq,
                  pl.BlockSpec((1, 2, nb, LANES, t), lambda bi, g: (bi, g, 0, 0, 0)),
                  _const_spec(t, t)],
        out_specs=pl.BlockSpec((1, s, LANES), lambda bi, g: (bi, 0, g)),
        out_shape=jax.ShapeDtypeStruct((b, s, ATT_W), F32),
        scratch_shapes=[pltpu.VMEM((2, 2, t, t), F32), pltpu.VMEM((2, nb, 1, t), F32),
                        pltpu.VMEM((2, nb, LANES, t), F32)],
        compiler_params=pltpu.CompilerParams(
            dimension_semantics=("arbitrary", "arbitrary"), vmem_limit_bytes=VMEM_LIMIT),
        name="attention",
    )(q, k, vt, bias)


def _out_kernel(o_ref, oc_ref, h_ref, gm_ref, gf_ref, wout_ref, out_ref):
    o = o_ref[0]
    half = ATT_W // 2
    mixed = jnp.concatenate([
        _rms(o[:, :half], gm_ref[...]).astype(BF16),
        oc_ref[0],
        _rms(o[:, half:], gf_ref[...]).astype(BF16)], axis=-1)
    out_ref[0] = h_ref[0] + _dot(mixed, wout_ref[...])


def _out_call(o, oc, h, layer, w, *, ts=TS_OUT):
    b, s, _ = h.shape
    tile = lambda width: pl.BlockSpec((1, ts, width), lambda bi, si: (bi, si, 0))
    return pl.pallas_call(
        _out_kernel,
        grid=(b, s // ts),
        in_specs=[tile(ATT_W), tile(CONV_WIDTH), tile(D_MODEL),
                  _layer_spec(layer, 1, ATT_W // 2), _layer_spec(layer, 1, ATT_W // 2),
                  _layer_spec(layer, D_MODEL, D_MODEL)],
        out_specs=tile(D_MODEL),
        out_shape=jax.ShapeDtypeStruct(h.shape, F32),
        compiler_params=pltpu.CompilerParams(
            dimension_semantics=("arbitrary", "arbitrary"), vmem_limit_bytes=VMEM_LIMIT),
        name="out_proj",
    )(o, oc, h, w["mla_out_norm"], w["fox_out_norm"], w["w_out"])


def _ffn_kernel(hp_ref, h_ref, g_ref, wg_ref, wv_ref, cwg_ref, cwv_ref, cbg_ref, cbv_ref, wd_ref,
                out_ref, m_sc, *, ts):
    c = pl.program_id(2)

    @pl.when(c == 0)
    def _():
        keep = (pl.program_id(1) > 0).astype(F32)
        m_sc[0:HALO, :] = _rms(hp_ref[0] * keep, g_ref[...]).astype(BF16)
        m_sc[HALO:HALO + ts, :] = _rms(h_ref[0], g_ref[...]).astype(BF16)

    m = m_sc[...]

    def conv(u, cw_ref, cb_ref):
        return (cw_ref[0:1, :] * u[HALO - 2:HALO - 2 + ts, :]
                + cw_ref[1:2, :] * u[HALO - 1:HALO - 1 + ts, :]
                + cw_ref[2:3, :] * u[HALO:HALO + ts, :] + cb_ref[...])

    gate = conv(_dot(m, wg_ref[...]), cwg_ref, cbg_ref)
    val = conv(_dot(m, wv_ref[...]), cwv_ref, cbv_ref)
    act = (gate * jax.nn.sigmoid(gate) * val).astype(BF16)
    part = _dot(act, wd_ref[...])

    @pl.when(c == 0)
    def _():
        out_ref[0] = h_ref[0] + part

    @pl.when(c > 0)
    def _():
        out_ref[0] += part


def _ffn_call(h, layer, w, *, ts=TS_FFN, tf=TF_FFN):
    b, s, _ = h.shape
    nf = D_FF // tf
    halo_blocks = ts // HALO
    cols = lambda rows, first: pl.BlockSpec((None, rows, tf), lambda bi, si, c: (layer, 0, first + c))
    return pl.pallas_call(
        functools.partial(_ffn_kernel, ts=ts),
        grid=(b, s // ts, nf),
        in_specs=[
            pl.BlockSpec((1, HALO, D_MODEL),
                         lambda bi, si, c: (bi, jnp.maximum(si * halo_blocks - 1, 0), 0)),
            pl.BlockSpec((1, ts, D_MODEL), lambda bi, si, c: (bi, si, 0)),
            _layer_spec(layer, 1, D_MODEL),
            cols(D_MODEL, 0), cols(D_MODEL, nf),
            cols(CONV_K, 0), cols(CONV_K, nf),
            cols(1, 0), cols(1, nf),
            pl.BlockSpec((None, tf, D_MODEL), lambda bi, si, c: (layer, c, 0)),
        ],
        out_specs=pl.BlockSpec((1, ts, D_MODEL), lambda bi, si, c: (bi, si, 0)),
        out_shape=jax.ShapeDtypeStruct(h.shape, F32),
        scratch_shapes=[pltpu.VMEM((ts + HALO, D_MODEL), BF16)],
        compiler_params=pltpu.CompilerParams(
            dimension_semantics=("arbitrary", "arbitrary", "arbitrary"),
            vmem_limit_bytes=VMEM_LIMIT),
        name="ffn",
    )(h, h, w["ffn_norm"], w["w_up"], w["w_up"], w["ffn_conv_w"], w["ffn_conv_w"],
      w["ffn_conv_b"], w["ffn_conv_b"], w["w_down"])


def _ple_kernel(h_ref, p_ref, g_ref, wpg_ref, wple_ref, fg_ref, out_ref, *, final):
    h = h_ref[0]
    gate = jax.nn.sigmoid(_dot(_rms(h, g_ref[...]).astype(BF16), wpg_ref[...]))
    out = h + gate * _dot(p_ref[0].astype(BF16), wple_ref[...])
    if final:
        out = _rms(out, fg_ref[...])
    out_ref[0] = out


def _ple_call(h, p, layer, w, *, final, ts=TS_PLE):
    b, s, _ = h.shape
    tile = pl.BlockSpec((1, ts, D_MODEL), lambda bi, si: (bi, si, 0))
    return pl.pallas_call(
        functools.partial(_ple_kernel, final=final),
        grid=(b, s // ts),
        in_specs=[tile,
                  pl.BlockSpec((None, 1, ts, PLE_DIM), lambda bi, si: (layer, bi, si, 0)),
                  _layer_spec(layer, 1, D_MODEL), _layer_spec(layer, D_MODEL, D_MODEL),
                  _layer_spec(layer, PLE_DIM, D_MODEL), _const_spec(1, D_MODEL)],
        out_specs=tile,
        out_shape=jax.ShapeDtypeStruct(h.shape, F32),
        compiler_params=pltpu.CompilerParams(
            dimension_semantics=("arbitrary", "arbitrary"), vmem_limit_bytes=VMEM_LIMIT),
        name="ple",
    )(h, p, w["ple_norm"], w["w_ple_gate"], w["w_ple"], w["final_norm"])


def _extras_base(h):
    return h * LANES + (HALF if h % 2 == 0 else 0)


def _placement(offset, sign):
    m = np.zeros((3, LANES, HW), np.float32)
    for t in range(3):
        for h in range(FOX_HEADS):
            m[t, h, _extras_base(h) + offset + t] = sign
    return jnp.asarray(m, BF16)


def _extra_ones():
    rows = np.zeros((2, HW), np.float32)
    for h in range(FOX_HEADS):
        rows[0, _extras_base(h) + 3:_extras_base(h) + 6] = 1.0
        rows[1, _extras_base(h):_extras_base(h) + 3] = 1.0
    return jnp.asarray(rows)


def _v_ones():
    col = np.zeros((HW, LANES), np.float32)
    for h in range(MLA_HEADS):
        col[_extras_base(h)] = 1.0
    return jnp.asarray(col)


def _pad_last(x, before, width):
    return jnp.pad(x, [(0, 0)] * (x.ndim - 1) + [(before, width - before - x.shape[-1])])


def _prepare_weights(attn_norm, w_in, b_forget, q_norm, w_uq, kv_norm, w_ukv, conv_w,
                     mla_out_norm, conv_out_norm, fox_out_norm, w_out, ffn_norm, w_up,
                     ffn_conv_w, ffn_conv_b, w_down, ple_norm, w_ple_gate, w_ple, final_norm):
    depth = w_in.shape[0]
    half = MLA_ROPE // 2
    wr = w_in[..., O_ZR:O_ZB]
    wr_swapped = jnp.concatenate([wr[..., half:], wr[..., :half]], axis=-1)
    w_in_p = jnp.concatenate([
        w_in[..., O_ZQ:O_ZR],
        _pad_last(wr, MLA_NOPE, LANES), _pad_last(wr_swapped, MLA_NOPE, LANES),
        w_in[..., O_ZB:O_FV],
        _pad_last(w_in[..., O_FF:O_END], 0, LANES)], axis=-1).astype(BF16)

    uq = w_uq.reshape(depth, Q_RANK, MLA_HEADS, MLA_NOPE + MLA_ROPE)
    uq_rot = jnp.concatenate([uq[..., MLA_NOPE + half:], uq[..., MLA_NOPE:MLA_NOPE + half]], axis=-1)
    w_uq_p = jnp.concatenate([
        _pad_last(uq, 0, LANES).reshape(depth, Q_RANK, HW),
        _pad_last(uq_rot, MLA_NOPE, LANES).reshape(depth, Q_RANK, HW)], axis=-1).astype(BF16)
    ukv = w_ukv.reshape(depth, KV_RANK, MLA_HEADS, MLA_NOPE + MLA_V)
    w_uk = _pad_last(ukv[..., :MLA_NOPE], 0, LANES).reshape(depth, KV_RANK, HW).astype(BF16)
    w_uv = ukv[..., MLA_NOPE:].reshape(depth, KV_RANK, MLA_HEADS * MLA_V)

    rows = lambda v: v.reshape(depth, 1, -1)
    return {
        "attn_norm": rows(attn_norm),
        "w_in": w_in_p,
        "q_norm": rows(q_norm),
        "w_uq": w_uq_p,
        "kv_norm": rows(kv_norm),
        "w_uk": w_uk,
        "w_uv_t": jnp.swapaxes(w_uv, 1, 2).astype(BF16),
        "w_fv_t": jnp.swapaxes(w_in[..., O_FV:O_FF], 1, 2).astype(BF16),
        "v_ones": _v_ones(),
        "conv_w": conv_w,
        "conv_out_norm": rows(conv_out_norm),
        "b_forget": rows(_pad_last(b_forget, 0, LANES)),
        "place_q": _placement(0, 1.0),
        "place_k": _placement(3, -1.0),
        "extra_ones": _extra_ones(),
        "mla_out_norm": rows(mla_out_norm),
        "fox_out_norm": rows(fox_out_norm),
        "w_out": w_out.astype(BF16),
        "ffn_norm": rows(ffn_norm),
        "w_up": w_up.astype(BF16),
        "ffn_conv_w": ffn_conv_w,
        "ffn_conv_b": rows(ffn_conv_b),
        "w_down": w_down.astype(BF16),
        "ple_norm": rows(ple_norm),
        "w_ple_gate": w_ple_gate.astype(BF16),
        "w_ple": w_ple.astype(BF16),
        "final_norm": final_norm.reshape(1, D_MODEL),
    }


def _rope_rows(positions):
    b, s = positions.shape
    n_freq = MLA_ROPE // 2
    inv_freq = ROPE_THETA ** (-jnp.arange(0, MLA_ROPE, 2, dtype=F32) / MLA_ROPE)
    ang = positions.astype(F32)[..., None] * inv_freq
    dense = lax.optimization_barrier(ang.reshape(b, s * n_freq // LANES, LANES))
    cos, sin = lax.optimization_barrier((jnp.cos(dense), jnp.sin(dense)))
    cos = cos.reshape(b, s, n_freq)
    sin = sin.reshape(b, s, n_freq)
    lead = (b, s, MLA_NOPE)
    tail = (b, s, LANES - MLA_NOPE - MLA_ROPE)
    cos_t = jnp.concatenate([jnp.ones(lead, F32), cos, cos, jnp.zeros(tail, F32)], axis=-1)
    sin_t = jnp.concatenate([jnp.zeros(lead, F32), -sin, sin, jnp.zeros(tail, F32)], axis=-1)
    return cos_t, sin_t


def kernel(x, p, positions, attn_norm, w_in, b_forget, q_norm, w_uq, kv_norm, w_ukv, conv_w,
           mla_out_norm, conv_out_norm, fox_out_norm, w_out, ffn_norm, w_up, ffn_conv_w,
           ffn_conv_b, w_down, ple_norm, w_ple_gate, w_ple, final_norm):
    depth = w_in.shape[0]
    cos_t, sin_t = _rope_rows(positions)
    w = _prepare_weights(attn_norm, w_in, b_forget, q_norm, w_uq, kv_norm, w_ukv, conv_w,
                         mla_out_norm, conv_out_norm, fox_out_norm, w_out, ffn_norm, w_up,
                         ffn_conv_w, ffn_conv_b, w_down, ple_norm, w_ple_gate, w_ple, final_norm)
    h = x
    for layer in range(depth):
        q, k, vt, oc = _in_call(h, cos_t, sin_t, layer, w)
        o = _attn_call(q, k, vt)
        h = _out_call(o, oc, h, layer, w)
        h = _ffn_call(h, layer, w)
        h = _ple_call(h, p, layer, w, final=(layer == depth - 1))
    return h
```

```python
import functools
import math

import jax
import jax.numpy as jnp
import numpy as np
from jax import lax
from jax.experimental import pallas as pl
from jax.experimental.pallas import tpu as pltpu

F32 = jnp.float32
BF16 = jnp.bfloat16

D_MODEL = 1024
MLA_HEADS = 6
MLA_NOPE = 64
MLA_ROPE = 32
MLA_V = 64
Q_RANK = 256
KV_RANK = 128
FOX_HEADS = 6
FOX_DIM = 64
FOX_W = FOX_HEADS * FOX_DIM
CONV_WIDTH = 256
CONV_K = 3
D_FF = 2816
PLE_DIM = 256
ROPE_THETA = 10000.0
EPS = 1e-6
LOG2E = math.log2(math.e)

IN_SIZES = (Q_RANK, KV_RANK, MLA_ROPE, CONV_WIDTH, CONV_WIDTH, CONV_WIDTH, FOX_W, FOX_W, FOX_W,
            FOX_HEADS)
O_ZQ, O_ZKV, O_ZR, O_ZB, O_ZC, O_ZH, O_FQ, O_FK, O_FV, O_FF, O_END = (
    sum(IN_SIZES[:n]) for n in range(len(IN_SIZES) + 1))

LANES = 128
HALF = LANES // 2
BF16_SUBLANES = 16
V7X_VMEM_BYTES = 64 * 1024 * 1024
VMEM_LIMIT = V7X_VMEM_BYTES - 8 * 1024 * 1024

HEADS = MLA_HEADS + FOX_HEADS
HW = MLA_HEADS * LANES
ATT_W = HEADS * 64

C_ZQ = 0
C_ZKV = C_ZQ + Q_RANK
C_ZR = C_ZKV + KV_RANK
C_ZRS = C_ZR + LANES
C_ZB = C_ZRS + LANES
C_ZC = C_ZB + CONV_WIDTH
C_ZH = C_ZC + CONV_WIDTH
C_FQ = C_ZH + CONV_WIDTH
C_FK = C_FQ + FOX_W
C_FF = C_FK + FOX_W
N_Z = C_FF + LANES

TS_IN = 512
T_ATT = 512
MASK_BIAS = -1e30
TS_POST = 512
HALO = BF16_SUBLANES
FF_CHUNKS = ((0, 768), (768, 768), (1536, 768), (2304, 512))


def _rms(x, g):
    return x * lax.rsqrt(jnp.mean(x * x, axis=-1, keepdims=True) + EPS) * g


def _split3(x):
    hi = x.astype(BF16)
    r = x - hi.astype(F32)
    mid = r.astype(BF16)
    lo = (r - mid.astype(F32)).astype(BF16)
    return hi, mid, lo


def _dot(a, b):
    return jnp.dot(a, b, preferred_element_type=F32)


def _dot_nt(a, b):
    return lax.dot_general(a, b, (((1,), (1,)), ((), ())), preferred_element_type=F32)


def _layer_spec(layer, *shape):
    return pl.BlockSpec((None,) + shape, lambda *_: (layer,) + (0,) * len(shape))


def _const_spec(*shape):
    return pl.BlockSpec(shape, lambda *_: (0,) * len(shape))


def _in_kernel(x_ref, g_ref, win_ref, qn_ref, wuq_ref, kvn_ref, wuk_ref, wuvt_ref, wfvt_ref,
               cos_ref, sin_ref, cw_ref, cn_ref, bf_ref, pq_ref, pk_ref, rows_ref, vones_ref,
               q_ref, k_ref, vt_ref, oc_ref, xs_sc, cum_sc, *, ts):
    @pl.when(pl.program_id(1) == 0)
    def _():
        xs_sc[0:8, :] = jnp.zeros((8, CONV_WIDTH), F32)
        cum_sc[...] = jnp.zeros_like(cum_sc)

    a = _rms(x_ref[0], g_ref[...]).astype(BF16)

    def proj(lo, width):
        return _dot(a, win_ref[:, lo:lo + width])

    cos_t = cos_ref[0]
    sin_t = sin_ref[0]
    low_lanes = lax.broadcasted_iota(jnp.int32, (ts, LANES), 1) < HALF
    top_rows = lax.broadcasted_iota(jnp.int32, (LANES, ts), 0) < HALF

    def head_blocks(h):
        return (slice(h * LANES, (h + 1) * LANES),
                slice((h // 2) * LANES, (h // 2 + 1) * LANES), h % 2 == 0)

    def v_row(pair_t, h):
        own, pair, even = head_blocks(h)
        mine = top_rows if even else jnp.logical_not(top_rows)
        ones = jnp.tile(vones_ref[own, :], (1, ts // LANES))
        return jnp.where(mine, pair_t[pair, :], ones).astype(BF16)

    qn = _rms(proj(C_ZQ, Q_RANK), qn_ref[...]).astype(BF16)
    q_main = _dot(qn, wuq_ref[:, 0:HW])
    q_swap = _dot(qn, wuq_ref[:, HW:2 * HW])
    mla_scale = LOG2E * (MLA_NOPE + MLA_ROPE) ** -0.5
    kvn = _rms(proj(C_ZKV, KV_RANK), kvn_ref[...]).astype(BF16)
    k_nope = _dot(kvn, wuk_ref[...])
    v_mla_t = _dot_nt(wuvt_ref[...], kvn)
    k_rope = proj(C_ZR, LANES) * cos_t + proj(C_ZRS, LANES) * sin_t
    for h in range(MLA_HEADS):
        own = head_blocks(h)[0]
        q_ref[0, h] = ((q_main[:, own] * cos_t + q_swap[:, own] * sin_t) * mla_scale).astype(BF16)
        k_ref[0, h] = (k_nope[:, own] + k_rope).astype(BF16)
        vt_ref[0, h, 0] = v_row(v_mla_t, h)

    xc = proj(C_ZC, CONV_WIDTH) * proj(C_ZH, CONV_WIDTH)
    xs_sc[8:8 + ts, :] = xc
    y = (cw_ref[0:1, :] * xs_sc[6:6 + ts, :] + cw_ref[1:2, :] * xs_sc[7:7 + ts, :]
         + cw_ref[2:3, :] * xc)
    oc_ref[0] = _rms(proj(C_ZB, CONV_WIDTH) * y, cn_ref[...]).astype(BF16)
    xs_sc[0:8, :] = xs_sc[ts:ts + 8, :]

    ff = proj(C_FF, LANES) + bf_ref[...]
    log_f = jnp.minimum(ff, 0.0) - jnp.log1p(jnp.exp(-jnp.abs(ff)))
    row = lax.broadcasted_iota(jnp.int32, (ts, ts), 0)
    col = lax.broadcasted_iota(jnp.int32, (ts, ts), 1)
    tri = (col <= row).astype(BF16)
    f_hi, f_mid, f_lo = _split3(log_f)
    cum = _dot(tri, f_hi) + _dot(tri, f_mid) + _dot(tri, f_lo) + cum_sc[0:1, :]
    cum_sc[0:1, :] = cum[ts - 1:ts, :]
    c_hi, c_mid, c_lo = _split3(cum * LOG2E)
    extra_q = (_dot(c_hi, pq_ref[0]) + _dot(c_mid, pq_ref[1]) + _dot(c_lo, pq_ref[2])
               + rows_ref[0:1, :])
    extra_k = (_dot(c_hi, pk_ref[0]) + _dot(c_mid, pk_ref[1]) + _dot(c_lo, pk_ref[2])
               + rows_ref[1:2, :])

    fq = proj(C_FQ, FOX_W) * (LOG2E * FOX_DIM ** -0.5)
    fk = proj(C_FK, FOX_W)
    fv_t = _dot_nt(wfvt_ref[...], a)
    for h in range(FOX_HEADS):
        own, pair, even = head_blocks(h)
        mine = low_lanes if even else jnp.logical_not(low_lanes)
        q_ref[0, MLA_HEADS + h] = jnp.where(mine, fq[:, pair], extra_q[:, own]).astype(BF16)
        k_ref[0, MLA_HEADS + h] = jnp.where(mine, fk[:, pair], extra_k[:, own]).astype(BF16)
        vt_ref[0, MLA_HEADS + h, 0] = v_row(fv_t, h)


def _in_call(h, cos_t, sin_t, layer, w, *, ts=TS_IN):
    b, s, _ = h.shape
    assert ts == T_ATT
    tile = lambda width: pl.BlockSpec((1, ts, width), lambda bi, si: (bi, si, 0))
    head_spec = pl.BlockSpec((1, HEADS, ts, LANES), lambda bi, si: (bi, 0, si, 0))
    head_shape = jax.ShapeDtypeStruct((b, HEADS, s, LANES), BF16)
    vt_spec = pl.BlockSpec((1, HEADS, 1, LANES, ts), lambda bi, si: (bi, 0, si, 0, 0))
    vt_shape = jax.ShapeDtypeStruct((b, HEADS, s // ts, LANES, ts), BF16)
    return pl.pallas_call(
        functools.partial(_in_kernel, ts=ts),
        grid=(b, s // ts),
        in_specs=[
            tile(D_MODEL),
            _layer_spec(layer, 1, D_MODEL),
            _layer_spec(layer, D_MODEL, N_Z),
            _layer_spec(layer, 1, Q_RANK),
            _layer_spec(layer, Q_RANK, 2 * HW),
            _layer_spec(layer, 1, KV_RANK),
            _layer_spec(layer, KV_RANK, HW),
            _layer_spec(layer, MLA_HEADS * MLA_V, KV_RANK),
            _layer_spec(layer, FOX_W, D_MODEL),
            tile(LANES),
            tile(LANES),
            _layer_spec(layer, CONV_K, CONV_WIDTH),
            _layer_spec(layer, 1, CONV_WIDTH),
            _layer_spec(layer, 1, LANES),
            _const_spec(3, LANES, HW),
            _const_spec(3, LANES, HW),
            _const_spec(2, HW),
            _const_spec(HW, LANES),
        ],
        out_specs=[head_spec, head_spec, vt_spec, tile(CONV_WIDTH)],
        out_shape=[head_shape, head_shape, vt_shape,
                   jax.ShapeDtypeStruct((b, s, CONV_WIDTH), BF16)],
        scratch_shapes=[pltpu.VMEM((ts + 8, CONV_WIDTH), F32), pltpu.VMEM((8, LANES), F32)],
        compiler_params=pltpu.CompilerParams(
            dimension_semantics=("arbitrary", "arbitrary"), vmem_limit_bytes=VMEM_LIMIT),
        name="in_proj",
    )(h, w["attn_norm"], w["w_in"], w["q_norm"], w["w_uq"], w["kv_norm"], w["w_uk"], w["w_uv_t"],
      w["w_fv_t"], cos_t, sin_t, w["conv_w"], w["conv_out_norm"], w["b_forget"], w["place_q"],
      w["place_k"], w["extra_ones"], w["v_ones"])


def _attn_kernel(q_ref, k_ref, vt_ref, bias_ref, o_ref, s_sc, m_sc, acc_sc, *, t, nb):
    m_sc[...] = jnp.full_like(m_sc, -jnp.inf)
    acc_sc[...] = jnp.zeros_like(acc_sc)
    row = lax.broadcasted_iota(jnp.int32, (LANES, t), 0)

    def chunk(ref, hh, j):
        return ref[0, hh, pl.ds(pl.multiple_of(j * t, t), t), :]

    def scores(hh, i, j, slot, diag):
        s_t = _dot_nt(chunk(k_ref, hh, j), chunk(q_ref, hh, i))
        s_sc[slot, hh] = s_t + bias_ref[...] if diag else s_t

    def consume(hh, i, j, slot):
        m_prev = m_sc[hh, i]
        m_new = jnp.maximum(m_prev, jnp.max(s_sc[slot, hh], axis=0, keepdims=True))
        alpha = jnp.exp2(m_prev - m_new)
        p_t = jnp.exp2(s_sc[slot, hh] - m_new).astype(BF16)
        acc = alpha * acc_sc[hh, i] + _dot(vt_ref[0, hh, j], p_t)
        acc_sc[hh, i] = acc
        m_sc[hh, i] = m_new
        return acc

    def below(i, j, slot):
        wrap = j + 1 == i
        ni = jnp.where(wrap, jnp.minimum(i + 1, nb - 1), i)
        nj = jnp.where(wrap, 0, j + 1)
        for hh in range(2):
            scores(hh, ni, nj, 1 - slot, False)
        for hh in range(2):
            consume(hh, i, j, slot)
        return ni, nj

    first = (jnp.int32(1), jnp.int32(0))
    for hh in range(2):
        scores(hh, *first, 0, False)
    n_below = nb * (nb - 1) // 2
    assert n_below % 2 == 0 and nb % 2 == 0
    lax.fori_loop(0, n_below // 2, lambda _, unit: below(*below(*unit, 0), 1), first)

    def diagonal(i, slot):
        ni = jnp.minimum(i + 1, nb - 1)
        for hh in range(2):
            scores(hh, ni, ni, 1 - slot, True)
        acc0 = consume(0, i, i, slot)
        acc1 = consume(1, i, i, slot)
        o_t = jnp.where(row < HALF, acc0 / acc0[HALF:HALF + 1, :], acc1 / acc1[0:1, :])
        o_ref[0, pl.ds(pl.multiple_of(i * t, t), t), :] = o_t.T
        return ni

    zero = jnp.int32(0)
    for hh in range(2):
        scores(hh, zero, zero, 0, True)
    lax.fori_loop(0, nb // 2, lambda _, i: diagonal(diagonal(i, 0), 1), zero)


def _attn_call(q, k, vt, *, t=T_ATT):
    b, _, s, _ = q.shape
    nb = s // t
    key = lax.broadcasted_iota(jnp.int32, (t, t), 0)
    query = lax.broadcasted_iota(jnp.int32, (t, t), 1)
    bias = jnp.where(key <= query, 0.0, MASK_BIAS).astype(F32)
    seq = pl.BlockSpec((1, 2, s, LANES), lambda bi, g: (bi, g, 0, 0))
    return pl.pallas_call(
        functools.partial(_attn_kernel, t=t, nb=nb),
        grid=(b, HEADS // 2),
        in_specs=[seq, seq,
                  pl.BlockSpec((1, 2, nb, LANES, t), lambda bi, g: (bi, g, 0, 0, 0)),
                  _const_spec(t, t)],
        out_specs=pl.BlockSpec((1, s, LANES), lambda bi, g: (bi, 0, g)),
        out_shape=jax.ShapeDtypeStruct((b, s, ATT_W), F32),
        scratch_shapes=[pltpu.VMEM((2, 2, t, t), F32), pltpu.VMEM((2, nb, 1, t), F32),
                        pltpu.VMEM((2, nb, LANES, t), F32)],
        compiler_params=pltpu.CompilerParams(
            dimension_semantics=("arbitrary", "arbitrary"), vmem_limit_bytes=VMEM_LIMIT),
        name="attention",
    )(q, k, vt, bias)


def _post_kernel(o_ref, op_ref, oc_ref, ocp_ref, h_ref, hp_ref, p_ref, gm_ref, gf_ref, wout_ref,
                 gffn_ref, wup_ref, cw_ref, cb_ref, wd_ref, gple_ref, wpg_ref, wple_ref, fg_ref,
                 out_ref, h1_sc, m_sc, act_sc, *, ts, final):
    half = ATT_W // 2

    def mix(o, oc, h):
        mixed = jnp.concatenate([_rms(o[:, :half], gm_ref[...]).astype(BF16), oc,
                                 _rms(o[:, half:], gf_ref[...]).astype(BF16)], axis=-1)
        return h + _dot(mixed, wout_ref[...])

    keep = (pl.program_id(1) > 0).astype(F32)
    h1_prev = mix(op_ref[0], ocp_ref[0], hp_ref[0]) * keep
    h1 = mix(o_ref[0], oc_ref[0], h_ref[0])
    h1_sc[...] = h1
    m_sc[0:HALO, :] = _rms(h1_prev, gffn_ref[...]).astype(BF16)
    m_sc[HALO:HALO + ts, :] = _rms(h1, gffn_ref[...]).astype(BF16)

    def up_conv(lo, width):
        u = _dot(m_sc[...], wup_ref[:, lo:lo + width])
        return (cw_ref[0:1, lo:lo + width] * u[HALO - 2:HALO - 2 + ts, :]
                + cw_ref[1:2, lo:lo + width] * u[HALO - 1:HALO - 1 + ts, :]
                + cw_ref[2:3, lo:lo + width] * u[HALO:HALO + ts, :] + cb_ref[:, lo:lo + width])

    for lo, width in FF_CHUNKS:
        gate = up_conv(lo, width)
        val = up_conv(D_FF + lo, width)
        act_sc[:, lo:lo + width] = (gate * jax.nn.sigmoid(gate) * val).astype(BF16)
    h2 = h1_sc[...] + _dot(act_sc[...], wd_ref[...])

    gate = jax.nn.sigmoid(_dot(_rms(h2, gple_ref[...]).astype(BF16), wpg_ref[...]))
    out = h2 + gate * _dot(p_ref[0].astype(BF16), wple_ref[...])
    if final:
        out = _rms(out, fg_ref[...])
    out_ref[0] = out


def _post_call(o, oc, h, p, layer, w, *, final, ts=TS_POST):
    b, s, _ = h.shape
    halo_blocks = ts // HALO
    tile = lambda width: pl.BlockSpec((1, ts, width), lambda bi, si: (bi, si, 0))
    halo = lambda width: pl.BlockSpec(
        (1, HALO, width), lambda bi, si: (bi, jnp.maximum(si * halo_blocks - 1, 0), 0))

    def resident(*shape):
        return pl.BlockSpec((None,) + shape, lambda bi, si: (layer,) + (0,) * len(shape),
                            pipeline_mode=pl.Buffered(1))

    return pl.pallas_call(
        functools.partial(_post_kernel, ts=ts, final=final),
        grid=(b, s // ts),
        in_specs=[
            tile(ATT_W), halo(ATT_W), tile(CONV_WIDTH), halo(CONV_WIDTH), tile(D_MODEL),
            halo(D_MODEL),
            pl.BlockSpec((None, 1, ts, PLE_DIM), lambda bi, si: (layer, bi, si, 0)),
            _layer_spec(layer, 1, ATT_W // 2), _layer_spec(layer, 1, ATT_W // 2),
            resident(D_MODEL, D_MODEL),
            _layer_spec(layer, 1, D_MODEL), resident(D_MODEL, 2 * D_FF),
            _layer_spec(layer, CONV_K, 2 * D_FF), _layer_spec(layer, 1, 2 * D_FF),
            resident(D_FF, D_MODEL),
            _layer_spec(layer, 1, D_MODEL), resident(D_MODEL, D_MODEL), resident(PLE_DIM, D_MODEL),
            _const_spec(1, D_MODEL),
        ],
        out_specs=tile(D_MODEL),
        out_shape=jax.ShapeDtypeStruct(h.shape, F32),
        scratch_shapes=[pltpu.VMEM((ts, D_MODEL), F32), pltpu.VMEM((ts + HALO, D_MODEL), BF16),
                        pltpu.VMEM((ts, D_FF), BF16)],
        compiler_params=pltpu.CompilerParams(
            dimension_semantics=("arbitrary", "arbitrary"), vmem_limit_bytes=VMEM_LIMIT),
        name="post_attention",
    )(o, o, oc, oc, h, h, p, w["mla_out_norm"], w["fox_out_norm"], w["w_out"], w["ffn_norm"],
      w["w_up"], w["ffn_conv_w"], w["ffn_conv_b"], w["w_down"], w["ple_norm"], w["w_ple_gate"],
      w["w_ple"], w["final_norm"])


def _extras_base(h):
    return h * LANES + (HALF if h % 2 == 0 else 0)


def _placement(offset, sign):
    m = np.zeros((3, LANES, HW), np.float32)
    for t in range(3):
        for h in range(FOX_HEADS):
            m[t, h, _extras_base(h) + offset + t] = sign
    return jnp.asarray(m, BF16)


def _extra_ones():
    rows = np.zeros((2, HW), np.float32)
    for h in range(FOX_HEADS):
        rows[0, _extras_base(h) + 3:_extras_base(h) + 6] = 1.0
        rows[1, _extras_base(h):_extras_base(h) + 3] = 1.0
    return jnp.asarray(rows)


def _v_ones():
    col = np.zeros((HW, LANES), np.float32)
    for h in range(MLA_HEADS):
        col[_extras_base(h)] = 1.0
    return jnp.asarray(col)


def _pad_last(x, before, width):
    return jnp.pad(x, [(0, 0)] * (x.ndim - 1) + [(before, width - before - x.shape[-1])])


def _prepare_weights(attn_norm, w_in, b_forget, q_norm, w_uq, kv_norm, w_ukv, conv_w,
                     mla_out_norm, conv_out_norm, fox_out_norm, w_out, ffn_norm, w_up,
                     ffn_conv_w, ffn_conv_b, w_down, ple_norm, w_ple_gate, w_ple, final_norm):
    depth = w_in.shape[0]
    half = MLA_ROPE // 2
    wr = w_in[..., O_ZR:O_ZB]
    wr_swapped = jnp.concatenate([wr[..., half:], wr[..., :half]], axis=-1)
    w_in_p = jnp.concatenate([
        w_in[..., O_ZQ:O_ZR],
        _pad_last(wr, MLA_NOPE, LANES), _pad_last(wr_swapped, MLA_NOPE, LANES),
        w_in[..., O_ZB:O_FV],
        _pad_last(w_in[..., O_FF:O_END], 0, LANES)], axis=-1).astype(BF16)

    uq = w_uq.reshape(depth, Q_RANK, MLA_HEADS, MLA_NOPE + MLA_ROPE)
    uq_rot = jnp.concatenate([uq[..., MLA_NOPE + half:], uq[..., MLA_NOPE:MLA_NOPE + half]], axis=-1)
    w_uq_p = jnp.concatenate([
        _pad_last(uq, 0, LANES).reshape(depth, Q_RANK, HW),
        _pad_last(uq_rot, MLA_NOPE, LANES).reshape(depth, Q_RANK, HW)], axis=-1).astype(BF16)
    ukv = w_ukv.reshape(depth, KV_RANK, MLA_HEADS, MLA_NOPE + MLA_V)
    w_uk = _pad_last(ukv[..., :MLA_NOPE], 0, LANES).reshape(depth, KV_RANK, HW).astype(BF16)
    w_uv = ukv[..., MLA_NOPE:].reshape(depth, KV_RANK, MLA_HEADS * MLA_V)

    rows = lambda v: v.reshape(depth, 1, -1)
    return {
        "attn_norm": rows(attn_norm),
        "w_in": w_in_p,
        "q_norm": rows(q_norm),
        "w_uq": w_uq_p,
        "kv_norm": rows(kv_norm),
        "w_uk": w_uk,
        "w_uv_t": jnp.swapaxes(w_uv, 1, 2).astype(BF16),
        "w_fv_t": jnp.swapaxes(w_in[..., O_FV:O_FF], 1, 2).astype(BF16),
        "v_ones": _v_ones(),
        "conv_w": conv_w,
        "conv_out_norm": rows(conv_out_norm),
        "b_forget": rows(_pad_last(b_forget, 0, LANES)),
        "place_q": _placement(0, 1.0),
        "place_k": _placement(3, -1.0),
        "extra_ones": _extra_ones(),
        "mla_out_norm": rows(mla_out_norm),
        "fox_out_norm": rows(fox_out_norm),
        "w_out": w_out.astype(BF16),
        "ffn_norm": rows(ffn_norm),
        "w_up": w_up.astype(BF16),
        "ffn_conv_w": ffn_conv_w,
        "ffn_conv_b": rows(ffn_conv_b),
        "w_down": w_down.astype(BF16),
        "ple_norm": rows(ple_norm),
        "w_ple_gate": w_ple_gate.astype(BF16),
        "w_ple": w_ple.astype(BF16),
        "final_norm": final_norm.reshape(1, D_MODEL),
    }


def _rope_rows(positions):
    b, s = positions.shape
    n_freq = MLA_ROPE // 2
    inv_freq = ROPE_THETA ** (-jnp.arange(0, MLA_ROPE, 2, dtype=F32) / MLA_ROPE)
    ang = positions.astype(F32)[..., None] * inv_freq
    dense = lax.optimization_barrier(ang.reshape(b, s * n_freq // LANES, LANES))
    cos, sin = lax.optimization_barrier((jnp.cos(dense), jnp.sin(dense)))
    cos = cos.reshape(b, s, n_freq)
    sin = sin.reshape(b, s, n_freq)
    lead = (b, s, MLA_NOPE)
    tail = (b, s, LANES - MLA_NOPE - MLA_ROPE)
    cos_t = jnp.concatenate([jnp.ones(lead, F32), cos, cos, jnp.zeros(tail, F32)], axis=-1)
    sin_t = jnp.concatenate([jnp.zeros(lead, F32), -sin, sin, jnp.zeros(tail, F32)], axis=-1)
    return cos_t, sin_t


def kernel(x, p, positions, attn_norm, w_in, b_forget, q_norm, w_uq, kv_norm, w_ukv, conv_w,
           mla_out_norm, conv_out_norm, fox_out_norm, w_out, ffn_norm, w_up, ffn_conv_w,
           ffn_conv_b, w_down, ple_norm, w_ple_gate, w_ple, final_norm):
    depth = w_in.shape[0]
    cos_t, sin_t = _rope_rows(positions)
    w = _prepare_weights(attn_norm, w_in, b_forget, q_norm, w_uq, kv_norm, w_ukv, conv_w,
                         mla_out_norm, conv_out_norm, fox_out_norm, w_out, ffn_norm, w_up,
                         ffn_conv_w, ffn_conv_b, w_down, ple_norm, w_ple_gate, w_ple, final_norm)
    h = x
    for layer in range(depth):
        q, k, vt, oc = _in_call(h, cos_t, sin_t, layer, w)
        o = _attn_call(q, k, vt)
        h = _post_call(o, oc, h, p, layer, w, final=(layer == depth - 1))
    return h
```

```python
import functools
import math

import jax
import jax.numpy as jnp
import numpy as np
from jax import lax
from jax.experimental import pallas as pl
from jax.experimental.pallas import tpu as pltpu

F32 = jnp.float32
BF16 = jnp.bfloat16

D_MODEL = 1024
MLA_HEADS = 6
MLA_NOPE = 64
MLA_ROPE = 32
MLA_V = 64
Q_RANK = 256
KV_RANK = 128
FOX_HEADS = 6
FOX_DIM = 64
FOX_W = FOX_HEADS * FOX_DIM
CONV_WIDTH = 256
CONV_K = 3
D_FF = 2816
PLE_DIM = 256
ROPE_THETA = 10000.0
EPS = 1e-6
LOG2E = math.log2(math.e)

IN_SIZES = (Q_RANK, KV_RANK, MLA_ROPE, CONV_WIDTH, CONV_WIDTH, CONV_WIDTH, FOX_W, FOX_W, FOX_W,
            FOX_HEADS)
O_ZQ, O_ZKV, O_ZR, O_ZB, O_ZC, O_ZH, O_FQ, O_FK, O_FV, O_FF, O_END = (
    sum(IN_SIZES[:n]) for n in range(len(IN_SIZES) + 1))

LANES = 128
HALF = LANES // 2
BF16_SUBLANES = 16
V7X_VMEM_BYTES = 64 * 1024 * 1024
VMEM_LIMIT = V7X_VMEM_BYTES - 8 * 1024 * 1024

HEADS = MLA_HEADS + FOX_HEADS
HW = MLA_HEADS * LANES
ATT_W = HEADS * 64

C_ZQ = 0
C_ZKV = C_ZQ + Q_RANK
C_ZR = C_ZKV + KV_RANK
C_ZRS = C_ZR + LANES
C_ZB = C_ZRS + LANES
C_ZC = C_ZB + CONV_WIDTH
C_ZH = C_ZC + CONV_WIDTH
C_FQ = C_ZH + CONV_WIDTH
C_FK = C_FQ + FOX_W
C_FF = C_FK + FOX_W
N_Z = C_FF + LANES

TS_IN = 512
T_ATT = 512
MASK_BIAS = -1e30
TS_POST = 512
HALO = BF16_SUBLANES
FF_CHUNKS = ((0, 768), (768, 768), (1536, 768), (2304, 512))


def _rms(x, g):
    return x * lax.rsqrt(jnp.mean(x * x, axis=-1, keepdims=True) + EPS) * g


def _split3(x):
    hi = x.astype(BF16)
    r = x - hi.astype(F32)
    mid = r.astype(BF16)
    lo = (r - mid.astype(F32)).astype(BF16)
    return hi, mid, lo


def _dot(a, b):
    return jnp.dot(a, b, preferred_element_type=F32)


def _dot_nt(a, b):
    return lax.dot_general(a, b, (((1,), (1,)), ((), ())), preferred_element_type=F32)


def _layer_spec(layer, *shape):
    return pl.BlockSpec((None,) + shape, lambda *_: (layer,) + (0,) * len(shape))


def _const_spec(*shape):
    return pl.BlockSpec(shape, lambda *_: (0,) * len(shape))


def _in_kernel(x_ref, g_ref, win_ref, qn_ref, wuq_ref, kvn_ref, wuk_ref, wuvt_ref, wfvt_ref,
               cos_ref, sin_ref, cw_ref, cn_ref, bf_ref, place_ref, ones_ref, vones_ref,
               q_ref, k_ref, vt_ref, oc_ref, xs_sc, cum_sc, *, ts):
    @pl.when(pl.program_id(1) == 0)
    def _():
        xs_sc[0:8, :] = jnp.zeros((8, CONV_WIDTH), F32)
        cum_sc[...] = jnp.zeros_like(cum_sc)

    a = _rms(x_ref[0], g_ref[...]).astype(BF16)
    z = _dot(a, win_ref[...])

    def proj(lo, width):
        return z[:, lo:lo + width]

    cos_t = cos_ref[0]
    sin_t = sin_ref[0]
    low_lanes = lax.broadcasted_iota(jnp.int32, (ts, LANES), 1) < HALF
    top_rows = lax.broadcasted_iota(jnp.int32, (LANES, ts), 0) < HALF

    def head_blocks(h):
        return (slice(h * LANES, (h + 1) * LANES),
                slice((h // 2) * LANES, (h // 2 + 1) * LANES), h % 2 == 0)

    def v_row(pair_t, h):
        own, pair, even = head_blocks(h)
        mine = top_rows if even else jnp.logical_not(top_rows)
        ones = jnp.tile(vones_ref[own, :], (1, ts // LANES))
        return jnp.where(mine, pair_t[pair, :], ones).astype(BF16)

    qn = _rms(proj(C_ZQ, Q_RANK), qn_ref[...]).astype(BF16)
    q_main = _dot(qn, wuq_ref[:, 0:HW])
    q_swap = _dot(qn, wuq_ref[:, HW:2 * HW])
    mla_scale = LOG2E * (MLA_NOPE + MLA_ROPE) ** -0.5
    kvn = _rms(proj(C_ZKV, KV_RANK), kvn_ref[...]).astype(BF16)
    k_nope = _dot(kvn, wuk_ref[...])
    v_mla_t = _dot_nt(wuvt_ref[...], kvn)
    k_rope = proj(C_ZR, LANES) * cos_t + proj(C_ZRS, LANES) * sin_t
    for h in range(MLA_HEADS):
        own = head_blocks(h)[0]
        q_ref[0, h] = ((q_main[:, own] * cos_t + q_swap[:, own] * sin_t) * mla_scale).astype(BF16)
        k_ref[0, h] = (k_nope[:, own] + k_rope).astype(BF16)
        vt_ref[0, h, 0] = v_row(v_mla_t, h)

    xc = proj(C_ZC, CONV_WIDTH) * proj(C_ZH, CONV_WIDTH)
    xs_sc[8:8 + ts, :] = xc
    y = (cw_ref[0:1, :] * xs_sc[6:6 + ts, :] + cw_ref[1:2, :] * xs_sc[7:7 + ts, :]
         + cw_ref[2:3, :] * xc)
    oc_ref[0] = _rms(proj(C_ZB, CONV_WIDTH) * y, cn_ref[...]).astype(BF16)
    xs_sc[0:8, :] = xs_sc[ts:ts + 8, :]

    ff = proj(C_FF, LANES) + bf_ref[...]
    log_f = jnp.minimum(ff, 0.0) - jnp.log1p(jnp.exp(-jnp.abs(ff)))
    row = lax.broadcasted_iota(jnp.int32, (ts, ts), 0)
    col = lax.broadcasted_iota(jnp.int32, (ts, ts), 1)
    tri = (col <= row).astype(BF16)
    f_hi, f_mid, f_lo = _split3(log_f)
    cum = _dot(tri, f_hi) + _dot(tri, f_mid) + _dot(tri, f_lo) + cum_sc[0:1, :]
    cum_sc[0:1, :] = cum[ts - 1:ts, :]
    c_hi, c_mid, c_lo = (term.astype(F32) for term in _split3(cum * LOG2E))
    lane = lax.broadcasted_iota(jnp.int32, (ts, LANES), 1)
    nh = FOX_HEADS
    terms = jnp.where(lane < nh, c_hi,
                      jnp.where(lane < 2 * nh, pltpu.roll(c_mid, nh, axis=1),
                                jnp.where(lane < 3 * nh, pltpu.roll(c_lo, 2 * nh, axis=1), 0.0)))
    extras = _dot(terms.astype(BF16), place_ref[...]) + ones_ref[...]
    extra_q = extras[:, :HW]
    extra_k = extras[:, HW:]

    fq = proj(C_FQ, FOX_W) * (LOG2E * FOX_DIM ** -0.5)
    fk = proj(C_FK, FOX_W)
    fv_t = _dot_nt(wfvt_ref[...], a)
    for h in range(FOX_HEADS):
        own, pair, even = head_blocks(h)
        mine = low_lanes if even else jnp.logical_not(low_lanes)
        q_ref[0, MLA_HEADS + h] = jnp.where(mine, fq[:, pair], extra_q[:, own]).astype(BF16)
        k_ref[0, MLA_HEADS + h] = jnp.where(mine, fk[:, pair], extra_k[:, own]).astype(BF16)
        vt_ref[0, MLA_HEADS + h, 0] = v_row(fv_t, h)


def _in_call(h, cos_t, sin_t, layer, w, *, ts=TS_IN):
    b, s, _ = h.shape
    assert ts == T_ATT
    tile = lambda width: pl.BlockSpec((1, ts, width), lambda bi, si: (bi, si, 0))
    head_spec = pl.BlockSpec((1, HEADS, ts, LANES), lambda bi, si: (bi, 0, si, 0))
    head_shape = jax.ShapeDtypeStruct((b, HEADS, s, LANES), BF16)
    vt_spec = pl.BlockSpec((1, HEADS, 1, LANES, ts), lambda bi, si: (bi, 0, si, 0, 0))
    vt_shape = jax.ShapeDtypeStruct((b, HEADS, s // ts, LANES, ts), BF16)
    return pl.pallas_call(
        functools.partial(_in_kernel, ts=ts),
        grid=(b, s // ts),
        in_specs=[
            tile(D_MODEL),
            _layer_spec(layer, 1, D_MODEL),
            _layer_spec(layer, D_MODEL, N_Z),
            _layer_spec(layer, 1, Q_RANK),
            _layer_spec(layer, Q_RANK, 2 * HW),
            _layer_spec(layer, 1, KV_RANK),
            _layer_spec(layer, KV_RANK, HW),
            _layer_spec(layer, MLA_HEADS * MLA_V, KV_RANK),
            _layer_spec(layer, FOX_W, D_MODEL),
            tile(LANES),
            tile(LANES),
            _layer_spec(layer, CONV_K, CONV_WIDTH),
            _layer_spec(layer, 1, CONV_WIDTH),
            _layer_spec(layer, 1, LANES),
            _const_spec(LANES, 2 * HW),
            _const_spec(1, 2 * HW),
            _const_spec(HW, LANES),
        ],
        out_specs=[head_spec, head_spec, vt_spec, tile(CONV_WIDTH)],
        out_shape=[head_shape, head_shape, vt_shape,
                   jax.ShapeDtypeStruct((b, s, CONV_WIDTH), BF16)],
        scratch_shapes=[pltpu.VMEM((ts + 8, CONV_WIDTH), F32), pltpu.VMEM((8, LANES), F32)],
        compiler_params=pltpu.CompilerParams(
            dimension_semantics=("arbitrary", "arbitrary"), vmem_limit_bytes=VMEM_LIMIT),
        name="in_proj",
    )(h, w["attn_norm"], w["w_in"], w["q_norm"], w["w_uq"], w["kv_norm"], w["w_uk"], w["w_uv_t"],
      w["w_fv_t"], cos_t, sin_t, w["conv_w"], w["conv_out_norm"], w["b_forget"], w["place"],
      w["extra_ones"], w["v_ones"])


def _attn_kernel(q_ref, k_ref, vt_ref, bias_ref, o_ref, s_sc, m_sc, acc_sc, *, t, nb):
    m_sc[...] = jnp.full_like(m_sc, -jnp.inf)
    acc_sc[...] = jnp.zeros_like(acc_sc)
    row = lax.broadcasted_iota(jnp.int32, (LANES, t), 0)

    def chunk(ref, hh, j):
        return ref[0, hh, pl.ds(pl.multiple_of(j * t, t), t), :]

    def scores(hh, i, j, slot, diag):
        s_t = _dot_nt(chunk(k_ref, hh, j), chunk(q_ref, hh, i))
        s_sc[slot, hh] = s_t + bias_ref[...] if diag else s_t

    def consume(hh, i, j, slot):
        m_prev = m_sc[hh, i]
        m_new = jnp.maximum(m_prev, jnp.max(s_sc[slot, hh], axis=0, keepdims=True))
        alpha = jnp.exp2(m_prev - m_new)
        p_t = jnp.exp2(s_sc[slot, hh] - m_new).astype(BF16)
        acc = alpha * acc_sc[hh, i] + _dot(vt_ref[0, hh, j], p_t)
        acc_sc[hh, i] = acc
        m_sc[hh, i] = m_new
        return acc

    def below(i, j, slot):
        wrap = j + 1 == i
        ni = jnp.where(wrap, jnp.minimum(i + 1, nb - 1), i)
        nj = jnp.where(wrap, 0, j + 1)
        for hh in range(2):
            scores(hh, ni, nj, 1 - slot, False)
        for hh in range(2):
            consume(hh, i, j, slot)
        return ni, nj

    first = (jnp.int32(1), jnp.int32(0))
    for hh in range(2):
        scores(hh, *first, 0, False)
    n_below = nb * (nb - 1) // 2
    assert n_below % 2 == 0 and nb % 2 == 0
    lax.fori_loop(0, n_below // 2, lambda _, unit: below(*below(*unit, 0), 1), first)

    def diagonal(i, slot):
        ni = jnp.minimum(i + 1, nb - 1)
        for hh in range(2):
            scores(hh, ni, ni, 1 - slot, True)
        acc0 = consume(0, i, i, slot)
        acc1 = consume(1, i, i, slot)
        o_t = jnp.where(row < HALF, acc0 / acc0[HALF:HALF + 1, :], acc1 / acc1[0:1, :])
        o_ref[0, pl.ds(pl.multiple_of(i * t, t), t), :] = o_t.T
        return ni

    zero = jnp.int32(0)
    for hh in range(2):
        scores(hh, zero, zero, 0, True)
    lax.fori_loop(0, nb // 2, lambda _, i: diagonal(diagonal(i, 0), 1), zero)


def _attn_call(q, k, vt, *, t=T_ATT):
    b, _, s, _ = q.shape
    nb = s // t
    key = lax.broadcasted_iota(jnp.int32, (t, t), 0)
    query = lax.broadcasted_iota(jnp.int32, (t, t), 1)
    bias = jnp.where(key <= query, 0.0, MASK_BIAS).astype(F32)
    seq = pl.BlockSpec((1, 2, s, LANES), lambda bi, g: (bi, g, 0, 0))
    return pl.pallas_call(
        functools.partial(_attn_kernel, t=t, nb=nb),
        grid=(b, HEADS // 2),
        in_specs=[seq, seq,
                  pl.BlockSpec((1, 2, nb, LANES, t), lambda bi, g: (bi, g, 0, 0, 0)),
                  _const_spec(t, t)],
        out_specs=pl.BlockSpec((1, s, LANES), lambda bi, g: (bi, 0, g)),
        out_shape=jax.ShapeDtypeStruct((b, s, ATT_W), F32),
        scratch_shapes=[pltpu.VMEM((2, 2, t, t), F32), pltpu.VMEM((2, nb, 1, t), F32),
                        pltpu.VMEM((2, nb, LANES, t), F32)],
        compiler_params=pltpu.CompilerParams(
            dimension_semantics=("arbitrary", "arbitrary"), vmem_limit_bytes=VMEM_LIMIT),
        name="attention",
    )(q, k, vt, bias)


def _post_kernel(o_ref, op_ref, oc_ref, ocp_ref, h_ref, hp_ref, p_ref, gm_ref, gf_ref, wout_ref,
                 gffn_ref, wup_ref, cw_ref, cb_ref, wd_ref, gple_ref, wpg_ref, wple_ref, fg_ref,
                 out_ref, h1_sc, m_sc, act_sc, *, ts, final):
    half = ATT_W // 2

    def mix(o, oc, h):
        mixed = jnp.concatenate([_rms(o[:, :half], gm_ref[...]).astype(BF16), oc,
                                 _rms(o[:, half:], gf_ref[...]).astype(BF16)], axis=-1)
        return h + _dot(mixed, wout_ref[...])

    keep = (pl.program_id(1) > 0).astype(F32)
    h1_prev = mix(op_ref[0], ocp_ref[0], hp_ref[0]) * keep
    h1 = mix(o_ref[0], oc_ref[0], h_ref[0])
    h1_sc[...] = h1
    m_sc[0:HALO, :] = _rms(h1_prev, gffn_ref[...]).astype(BF16)
    m_sc[HALO:HALO + ts, :] = _rms(h1, gffn_ref[...]).astype(BF16)

    def up_conv(lo, width):
        u = _dot(m_sc[...], wup_ref[:, lo:lo + width])
        return (cw_ref[0:1, lo:lo + width] * u[HALO - 2:HALO - 2 + ts, :]
                + cw_ref[1:2, lo:lo + width] * u[HALO - 1:HALO - 1 + ts, :]
                + cw_ref[2:3, lo:lo + width] * u[HALO:HALO + ts, :] + cb_ref[:, lo:lo + width])

    for lo, width in FF_CHUNKS:
        gate = up_conv(lo, width)
        val = up_conv(D_FF + lo, width)
        act_sc[:, lo:lo + width] = (gate * jax.nn.sigmoid(gate) * val).astype(BF16)
    h2 = h1_sc[...] + _dot(act_sc[...], wd_ref[...])

    gate = jax.nn.sigmoid(_dot(_rms(h2, gple_ref[...]).astype(BF16), wpg_ref[...]))
    out = h2 + gate * _dot(p_ref[0].astype(BF16), wple_ref[...])
    if final:
        out = _rms(out, fg_ref[...])
    out_ref[0] = out


def _post_call(o, oc, h, p, layer, w, *, final, ts=TS_POST):
    b, s, _ = h.shape
    halo_blocks = ts // HALO
    tile = lambda width: pl.BlockSpec((1, ts, width), lambda bi, si: (bi, si, 0))
    halo = lambda width: pl.BlockSpec(
        (1, HALO, width), lambda bi, si: (bi, jnp.maximum(si * halo_blocks - 1, 0), 0))

    def resident(*shape):
        return pl.BlockSpec((None,) + shape, lambda bi, si: (layer,) + (0,) * len(shape),
                            pipeline_mode=pl.Buffered(1))

    return pl.pallas_call(
        functools.partial(_post_kernel, ts=ts, final=final),
        grid=(b, s // ts),
        in_specs=[
            tile(ATT_W), halo(ATT_W), tile(CONV_WIDTH), halo(CONV_WIDTH), tile(D_MODEL),
            halo(D_MODEL),
            pl.BlockSpec((None, 1, ts, PLE_DIM), lambda bi, si: (layer, bi, si, 0)),
            _layer_spec(layer, 1, ATT_W // 2), _layer_spec(layer, 1, ATT_W // 2),
            resident(D_MODEL, D_MODEL),
            _layer_spec(layer, 1, D_MODEL), resident(D_MODEL, 2 * D_FF),
            _layer_spec(layer, CONV_K, 2 * D_FF), _layer_spec(layer, 1, 2 * D_FF),
            resident(D_FF, D_MODEL),
            _layer_spec(layer, 1, D_MODEL), resident(D_MODEL, D_MODEL), resident(PLE_DIM, D_MODEL),
            _const_spec(1, D_MODEL),
        ],
        out_specs=tile(D_MODEL),
        out_shape=jax.ShapeDtypeStruct(h.shape, F32),
        scratch_shapes=[pltpu.VMEM((ts, D_MODEL), F32), pltpu.VMEM((ts + HALO, D_MODEL), BF16),
                        pltpu.VMEM((ts, D_FF), BF16)],
        compiler_params=pltpu.CompilerParams(
            dimension_semantics=("arbitrary", "arbitrary"), vmem_limit_bytes=VMEM_LIMIT),
        name="post_attention",
    )(o, o, oc, oc, h, h, p, w["mla_out_norm"], w["fox_out_norm"], w["w_out"], w["ffn_norm"],
      w["w_up"], w["ffn_conv_w"], w["ffn_conv_b"], w["w_down"], w["ple_norm"], w["w_ple_gate"],
      w["w_ple"], w["final_norm"])


def _extras_base(h):
    return h * LANES + (HALF if h % 2 == 0 else 0)


def _placement():
    m = np.zeros((LANES, 2 * HW), np.float32)
    for t in range(3):
        for h in range(FOX_HEADS):
            m[FOX_HEADS * t + h, _extras_base(h) + t] = 1.0
            m[FOX_HEADS * t + h, HW + _extras_base(h) + 3 + t] = -1.0
    return jnp.asarray(m, BF16)


def _extra_ones():
    row = np.zeros((1, 2 * HW), np.float32)
    for h in range(FOX_HEADS):
        row[0, _extras_base(h) + 3:_extras_base(h) + 6] = 1.0
        row[0, HW + _extras_base(h):HW + _extras_base(h) + 3] = 1.0
    return jnp.asarray(row)


def _v_ones():
    col = np.zeros((HW, LANES), np.float32)
    for h in range(MLA_HEADS):
        col[_extras_base(h)] = 1.0
    return jnp.asarray(col)


def _pad_last(x, before, width):
    return jnp.pad(x, [(0, 0)] * (x.ndim - 1) + [(before, width - before - x.shape[-1])])


def _prepare_weights(attn_norm, w_in, b_forget, q_norm, w_uq, kv_norm, w_ukv, conv_w,
                     mla_out_norm, conv_out_norm, fox_out_norm, w_out, ffn_norm, w_up,
                     ffn_conv_w, ffn_conv_b, w_down, ple_norm, w_ple_gate, w_ple, final_norm):
    depth = w_in.shape[0]
    half = MLA_ROPE // 2
    wr = w_in[..., O_ZR:O_ZB]
    wr_swapped = jnp.concatenate([wr[..., half:], wr[..., :half]], axis=-1)
    w_in_p = jnp.concatenate([
        w_in[..., O_ZQ:O_ZR],
        _pad_last(wr, MLA_NOPE, LANES), _pad_last(wr_swapped, MLA_NOPE, LANES),
        w_in[..., O_ZB:O_FV],
        _pad_last(w_in[..., O_FF:O_END], 0, LANES)], axis=-1).astype(BF16)

    uq = w_uq.reshape(depth, Q_RANK, MLA_HEADS, MLA_NOPE + MLA_ROPE)
    uq_rot = jnp.concatenate([uq[..., MLA_NOPE + half:], uq[..., MLA_NOPE:MLA_NOPE + half]], axis=-1)
    w_uq_p = jnp.concatenate([
        _pad_last(uq, 0, LANES).reshape(depth, Q_RANK, HW),
        _pad_last(uq_rot, MLA_NOPE, LANES).reshape(depth, Q_RANK, HW)], axis=-1).astype(BF16)
    ukv = w_ukv.reshape(depth, KV_RANK, MLA_HEADS, MLA_NOPE + MLA_V)
    w_uk = _pad_last(ukv[..., :MLA_NOPE], 0, LANES).reshape(depth, KV_RANK, HW).astype(BF16)
    w_uv = ukv[..., MLA_NOPE:].reshape(depth, KV_RANK, MLA_HEADS * MLA_V)

    rows = lambda v: v.reshape(depth, 1, -1)
    return {
        "attn_norm": rows(attn_norm),
        "w_in": w_in_p,
        "q_norm": rows(q_norm),
        "w_uq": w_uq_p,
        "kv_norm": rows(kv_norm),
        "w_uk": w_uk,
        "w_uv_t": jnp.swapaxes(w_uv, 1, 2).astype(BF16),
        "w_fv_t": jnp.swapaxes(w_in[..., O_FV:O_FF], 1, 2).astype(BF16),
        "v_ones": _v_ones(),
        "conv_w": conv_w,
        "conv_out_norm": rows(conv_out_norm),
        "b_forget": rows(_pad_last(b_forget, 0, LANES)),
        "place": _placement(),
        "extra_ones": _extra_ones(),
        "mla_out_norm": rows(mla_out_norm),
        "fox_out_norm": rows(fox_out_norm),
        "w_out": w_out.astype(BF16),
        "ffn_norm": rows(ffn_norm),
        "w_up": w_up.astype(BF16),
        "ffn_conv_w": ffn_conv_w,
        "ffn_conv_b": rows(ffn_conv_b),
        "w_down": w_down.astype(BF16),
        "ple_norm": rows(ple_norm),
        "w_ple_gate": w_ple_gate.astype(BF16),
        "w_ple": w_ple.astype(BF16),
        "final_norm": final_norm.reshape(1, D_MODEL),
    }


def _rope_rows(positions):
    b, s = positions.shape
    n_freq = MLA_ROPE // 2
    inv_freq = ROPE_THETA ** (-jnp.arange(0, MLA_ROPE, 2, dtype=F32) / MLA_ROPE)
    ang = positions.astype(F32)[..., None] * inv_freq
    dense = lax.optimization_barrier(ang.reshape(b, s * n_freq // LANES, LANES))
    cos, sin = lax.optimization_barrier((jnp.cos(dense), jnp.sin(dense)))
    cos = cos.reshape(b, s, n_freq)
    sin = sin.reshape(b, s, n_freq)
    lead = (b, s, MLA_NOPE)
    tail = (b, s, LANES - MLA_NOPE - MLA_ROPE)
    cos_t = jnp.concatenate([jnp.ones(lead, F32), cos, cos, jnp.zeros(tail, F32)], axis=-1)
    sin_t = jnp.concatenate([jnp.zeros(lead, F32), -sin, sin, jnp.zeros(tail, F32)], axis=-1)
    return cos_t, sin_t


def kernel(x, p, positions, attn_norm, w_in, b_forget, q_norm, w_uq, kv_norm, w_ukv, conv_w,
           mla_out_norm, conv_out_norm, fox_out_norm, w_out, ffn_norm, w_up, ffn_conv_w,
           ffn_conv_b, w_down, ple_norm, w_ple_gate, w_ple, final_norm):
    depth = w_in.shape[0]
    cos_t, sin_t = _rope_rows(positions)
    w = _prepare_weights(attn_norm, w_in, b_forget, q_norm, w_uq, kv_norm, w_ukv, conv_w,
                         mla_out_norm, conv_out_norm, fox_out_norm, w_out, ffn_norm, w_up,
                         ffn_conv_w, ffn_conv_b, w_down, ple_norm, w_ple_gate, w_ple, final_norm)
    h = x
    for layer in range(depth):
        q, k, vt, oc = _in_call(h, cos_t, sin_t, layer, w)
        o = _attn_call(q, k, vt)
        h = _post_call(o, oc, h, p, layer, w, final=(layer == depth - 1))
    return h
```

```python
import functools
import math

import jax
import jax.numpy as jnp
import numpy as np
from jax import lax
from jax.experimental import pallas as pl
from jax.experimental.pallas import tpu as pltpu

F32 = jnp.float32
BF16 = jnp.bfloat16

D_MODEL = 1024
MLA_HEADS = 6
MLA_NOPE = 64
MLA_ROPE = 32
MLA_V = 64
Q_RANK = 256
KV_RANK = 128
FOX_HEADS = 6
FOX_DIM = 64
FOX_W = FOX_HEADS * FOX_DIM
CONV_WIDTH = 256
CONV_K = 3
D_FF = 2816
PLE_DIM = 256
ROPE_THETA = 10000.0
EPS = 1e-6
LOG2E = math.log2(math.e)

IN_SIZES = (Q_RANK, KV_RANK, MLA_ROPE, CONV_WIDTH, CONV_WIDTH, CONV_WIDTH, FOX_W, FOX_W, FOX_W,
            FOX_HEADS)
O_ZQ, O_ZKV, O_ZR, O_ZB, O_ZC, O_ZH, O_FQ, O_FK, O_FV, O_FF, O_END = (
    sum(IN_SIZES[:n]) for n in range(len(IN_SIZES) + 1))

LANES = 128
HALF = LANES // 2
BF16_SUBLANES = 16
V7X_VMEM_BYTES = 64 * 1024 * 1024
VMEM_LIMIT = V7X_VMEM_BYTES - 8 * 1024 * 1024

HEADS = MLA_HEADS + FOX_HEADS
HW = MLA_HEADS * LANES
ATT_W = HEADS * 64

C_ZQ = 0
C_ZKV = C_ZQ + Q_RANK
C_ZR = C_ZKV + KV_RANK
C_ZRS = C_ZR + LANES
C_ZB = C_ZRS + LANES
C_ZC = C_ZB + CONV_WIDTH
C_ZH = C_ZC + CONV_WIDTH
C_FQ = C_ZH + CONV_WIDTH
C_FK = C_FQ + FOX_W
C_FF = C_FK + FOX_W
N_Z = C_FF + LANES

TS_IN = 512
T_ATT = 512
Q_SPLIT = 2
LEAD = 1
MASK_BIAS = -1e30
TS_POST = 512
HALO = BF16_SUBLANES
FF_CHUNKS = ((0, 768), (768, 768), (1536, 768), (2304, 512))


def _rms(x, g):
    return x * lax.rsqrt(jnp.mean(x * x, axis=-1, keepdims=True) + EPS) * g


def _split3(x):
    hi = x.astype(BF16)
    r = x - hi.astype(F32)
    mid = r.astype(BF16)
    lo = (r - mid.astype(F32)).astype(BF16)
    return hi, mid, lo


def _dot(a, b):
    return jnp.dot(a, b, preferred_element_type=F32)


def _dot_nt(a, b):
    return lax.dot_general(a, b, (((1,), (1,)), ((), ())), preferred_element_type=F32)


def _layer_spec(layer, *shape):
    return pl.BlockSpec((None,) + shape, lambda *_: (layer,) + (0,) * len(shape))


def _const_spec(*shape):
    return pl.BlockSpec(shape, lambda *_: (0,) * len(shape))


def _in_kernel(x_ref, g_ref, win_ref, qn_ref, wuq_ref, kvn_ref, wuk_ref, wuvt_ref, wfvt_ref,
               cos_ref, sin_ref, cw_ref, cn_ref, bf_ref, place_ref, ones_ref, vones_ref,
               q_ref, k_ref, vt_ref, oc_ref, xs_sc, cum_sc, *, ts):
    @pl.when(pl.program_id(1) == 0)
    def _():
        xs_sc[0:8, :] = jnp.zeros((8, CONV_WIDTH), F32)
        cum_sc[...] = jnp.zeros_like(cum_sc)

    a = _rms(x_ref[0], g_ref[...]).astype(BF16)
    z = _dot(a, win_ref[...])

    def proj(lo, width):
        return z[:, lo:lo + width]

    cos_t = cos_ref[0]
    sin_t = sin_ref[0]
    low_lanes = lax.broadcasted_iota(jnp.int32, (ts, LANES), 1) < HALF
    top_rows = lax.broadcasted_iota(jnp.int32, (LANES, ts), 0) < HALF

    def head_blocks(h):
        return (slice(h * LANES, (h + 1) * LANES),
                slice((h // 2) * LANES, (h // 2 + 1) * LANES), h % 2 == 0)

    def v_row(pair_t, h):
        own, pair, even = head_blocks(h)
        mine = top_rows if even else jnp.logical_not(top_rows)
        ones = jnp.tile(vones_ref[own, :], (1, ts // LANES))
        return jnp.where(mine, pair_t[pair, :], ones).astype(BF16)

    qn = _rms(proj(C_ZQ, Q_RANK), qn_ref[...]).astype(BF16)
    q_main = _dot(qn, wuq_ref[:, 0:HW])
    q_swap = _dot(qn, wuq_ref[:, HW:2 * HW])
    mla_scale = LOG2E * (MLA_NOPE + MLA_ROPE) ** -0.5
    kvn = _rms(proj(C_ZKV, KV_RANK), kvn_ref[...]).astype(BF16)
    k_nope = _dot(kvn, wuk_ref[...])
    v_mla_t = _dot_nt(wuvt_ref[...], kvn)
    k_rope = proj(C_ZR, LANES) * cos_t + proj(C_ZRS, LANES) * sin_t
    for h in range(MLA_HEADS):
        own = head_blocks(h)[0]
        q_ref[0, h] = ((q_main[:, own] * cos_t + q_swap[:, own] * sin_t) * mla_scale).astype(BF16)
        k_ref[0, h] = (k_nope[:, own] + k_rope).astype(BF16)
        vt_ref[0, h, 0] = v_row(v_mla_t, h)

    xc = proj(C_ZC, CONV_WIDTH) * proj(C_ZH, CONV_WIDTH)
    xs_sc[8:8 + ts, :] = xc
    y = (cw_ref[0:1, :] * xs_sc[6:6 + ts, :] + cw_ref[1:2, :] * xs_sc[7:7 + ts, :]
         + cw_ref[2:3, :] * xc)
    oc_ref[0] = _rms(proj(C_ZB, CONV_WIDTH) * y, cn_ref[...]).astype(BF16)
    xs_sc[0:8, :] = xs_sc[ts:ts + 8, :]

    ff = proj(C_FF, LANES) + bf_ref[...]
    log_f = jnp.minimum(ff, 0.0) - jnp.log1p(jnp.exp(-jnp.abs(ff)))
    row = lax.broadcasted_iota(jnp.int32, (ts, ts), 0)
    col = lax.broadcasted_iota(jnp.int32, (ts, ts), 1)
    tri = (col <= row).astype(BF16)
    f_hi, f_mid, f_lo = _split3(log_f)
    cum = _dot(tri, f_hi) + _dot(tri, f_mid) + _dot(tri, f_lo) + cum_sc[0:1, :]
    cum_sc[0:1, :] = cum[ts - 1:ts, :]
    c_hi, c_mid, c_lo = (term.astype(F32) for term in _split3(cum * LOG2E))
    lane = lax.broadcasted_iota(jnp.int32, (ts, LANES), 1)
    nh = FOX_HEADS
    terms = jnp.where(lane < nh, c_hi,
                      jnp.where(lane < 2 * nh, pltpu.roll(c_mid, nh, axis=1),
                                jnp.where(lane < 3 * nh, pltpu.roll(c_lo, 2 * nh, axis=1), 0.0)))
    extras = _dot(terms.astype(BF16), place_ref[...]) + ones_ref[...]
    extra_q = extras[:, :HW]
    extra_k = extras[:, HW:]

    fq = proj(C_FQ, FOX_W) * (LOG2E * FOX_DIM ** -0.5)
    fk = proj(C_FK, FOX_W)
    fv_t = _dot_nt(wfvt_ref[...], a)
    for h in range(FOX_HEADS):
        own, pair, even = head_blocks(h)
        mine = low_lanes if even else jnp.logical_not(low_lanes)
        q_ref[0, MLA_HEADS + h] = jnp.where(mine, fq[:, pair], extra_q[:, own]).astype(BF16)
        k_ref[0, MLA_HEADS + h] = jnp.where(mine, fk[:, pair], extra_k[:, own]).astype(BF16)
        vt_ref[0, MLA_HEADS + h, 0] = v_row(fv_t, h)


def _in_call(h, cos_t, sin_t, layer, w, *, ts=TS_IN):
    b, s, _ = h.shape
    assert ts == T_ATT
    tile = lambda width: pl.BlockSpec((1, ts, width), lambda bi, si: (bi, si, 0))
    head_spec = pl.BlockSpec((1, HEADS, ts, LANES), lambda bi, si: (bi, 0, si, 0))
    head_shape = jax.ShapeDtypeStruct((b, HEADS, s, LANES), BF16)
    vt_spec = pl.BlockSpec((1, HEADS, 1, LANES, ts), lambda bi, si: (bi, 0, si, 0, 0))
    vt_shape = jax.ShapeDtypeStruct((b, HEADS, s // ts, LANES, ts), BF16)
    return pl.pallas_call(
        functools.partial(_in_kernel, ts=ts),
        grid=(b, s // ts),
        in_specs=[
            tile(D_MODEL),
            _layer_spec(layer, 1, D_MODEL),
            _layer_spec(layer, D_MODEL, N_Z),
            _layer_spec(layer, 1, Q_RANK),
            _layer_spec(layer, Q_RANK, 2 * HW),
            _layer_spec(layer, 1, KV_RANK),
            _layer_spec(layer, KV_RANK, HW),
            _layer_spec(layer, MLA_HEADS * MLA_V, KV_RANK),
            _layer_spec(layer, FOX_W, D_MODEL),
            tile(LANES),
            tile(LANES),
            _layer_spec(layer, CONV_K, CONV_WIDTH),
            _layer_spec(layer, 1, CONV_WIDTH),
            _layer_spec(layer, 1, LANES),
            _const_spec(LANES, 2 * HW),
            _const_spec(1, 2 * HW),
            _const_spec(HW, LANES),
        ],
        out_specs=[head_spec, head_spec, vt_spec, tile(CONV_WIDTH)],
        out_shape=[head_shape, head_shape, vt_shape,
                   jax.ShapeDtypeStruct((b, s, CONV_WIDTH), BF16)],
        scratch_shapes=[pltpu.VMEM((ts + 8, CONV_WIDTH), F32), pltpu.VMEM((8, LANES), F32)],
        compiler_params=pltpu.CompilerParams(
            dimension_semantics=("arbitrary", "arbitrary"), vmem_limit_bytes=VMEM_LIMIT),
        name="in_proj",
    )(h, w["attn_norm"], w["w_in"], w["q_norm"], w["w_uq"], w["kv_norm"], w["w_uk"], w["w_uv_t"],
      w["w_fv_t"], cos_t, sin_t, w["conv_w"], w["conv_out_norm"], w["b_forget"], w["place"],
      w["extra_ones"], w["v_ones"])


def _attn_kernel(q_ref, k_ref, vt_ref, bias_ref, o_ref, s_sc, m_sc, acc_sc, *, t, nb):
    m_sc[...] = jnp.full_like(m_sc, -jnp.inf)
    acc_sc[...] = jnp.zeros_like(acc_sc)
    row = lax.broadcasted_iota(jnp.int32, (LANES, t), 0)

    tq = t // Q_SPLIT

    def scores(hh, i, j, slot, diag, part):
        cols = slice(part * tq, (part + 1) * tq)
        k = k_ref[0, hh, pl.ds(pl.multiple_of(j * t, t), t), :]
        q = q_ref[0, hh, pl.ds(pl.multiple_of(i * t + part * tq, tq), tq), :]
        s_t = _dot_nt(k, q)
        s_sc[slot, hh, :, cols] = s_t + bias_ref[:, cols] if diag else s_t

    def consume(hh, i, j, slot, part):
        cols = slice(part * tq, (part + 1) * tq)
        m_prev = m_sc[hh, i, :, cols]
        m_new = jnp.maximum(m_prev, jnp.max(s_sc[slot, hh, :, cols], axis=0, keepdims=True))
        alpha = jnp.exp2(m_prev - m_new)
        p_t = jnp.exp2(s_sc[slot, hh, :, cols] - m_new).astype(BF16)
        acc = alpha * acc_sc[hh, i, :, cols] + _dot(vt_ref[0, hh, j], p_t)
        acc_sc[hh, i, :, cols] = acc
        m_sc[hh, i, :, cols] = m_new
        return acc

    def staggered(i, j, ni, nj, slot, diag):
        pieces = [(hh, part) for hh in range(2) for part in range(Q_SPLIT)]
        for hh, part in pieces[:LEAD]:
            scores(hh, ni, nj, 1 - slot, diag, part)
        accs = [[], []]
        for n, (hh, part) in enumerate(pieces):
            if n + LEAD < len(pieces):
                scores(pieces[n + LEAD][0], ni, nj, 1 - slot, diag, pieces[n + LEAD][1])
            accs[hh].append(consume(hh, i, j, slot, part))
        return [jnp.concatenate(parts, axis=1) for parts in accs]

    def below(i, j, slot):
        wrap = j + 1 == i
        ni = jnp.where(wrap, jnp.minimum(i + 1, nb - 1), i)
        nj = jnp.where(wrap, 0, j + 1)
        staggered(i, j, ni, nj, slot, False)
        return ni, nj

    first = (jnp.int32(1), jnp.int32(0))
    for hh in range(2):
        for part in range(Q_SPLIT):
            scores(hh, *first, 0, False, part)
    n_below = nb * (nb - 1) // 2
    assert n_below % 2 == 0 and nb % 2 == 0
    lax.fori_loop(0, n_below // 2, lambda _, unit: below(*below(*unit, 0), 1), first)

    def diagonal(i, slot):
        ni = jnp.minimum(i + 1, nb - 1)
        acc0, acc1 = staggered(i, i, ni, ni, slot, True)
        o_t = jnp.where(row < HALF, acc0 / acc0[HALF:HALF + 1, :], acc1 / acc1[0:1, :])
        o_ref[0, pl.ds(pl.multiple_of(i * t, t), t), :] = o_t.T
        return ni

    zero = jnp.int32(0)
    for hh in range(2):
        for part in range(Q_SPLIT):
            scores(hh, zero, zero, 0, True, part)
    lax.fori_loop(0, nb // 2, lambda _, i: diagonal(diagonal(i, 0), 1), zero)


def _attn_call(q, k, vt, *, t=T_ATT):
    b, _, s, _ = q.shape
    nb = s // t
    key = lax.broadcasted_iota(jnp.int32, (t, t), 0)
    query = lax.broadcasted_iota(jnp.int32, (t, t), 1)
    bias = jnp.where(key <= query, 0.0, MASK_BIAS).astype(F32)
    seq = pl.BlockSpec((1, 2, s, LANES), lambda bi, g: (bi, g, 0, 0))
    return pl.pallas_call(
        functools.partial(_attn_kernel, t=t, nb=nb),
        grid=(b, HEADS // 2),
        in_specs=[seq, seq,
                  pl.BlockSpec((1, 2, nb, LANES, t), lambda bi, g: (bi, g, 0, 0, 0)),
                  _const_spec(t, t)],
        out_specs=pl.BlockSpec((1, s, LANES), lambda bi, g: (bi, 0, g)),
        out_shape=jax.ShapeDtypeStruct((b, s, ATT_W), F32),
        scratch_shapes=[pltpu.VMEM((2, 2, t, t), F32), pltpu.VMEM((2, nb, 1, t), F32),
                        pltpu.VMEM((2, nb, LANES, t), F32)],
        compiler_params=pltpu.CompilerParams(
            dimension_semantics=("arbitrary", "arbitrary"), vmem_limit_bytes=VMEM_LIMIT),
        name="attention",
    )(q, k, vt, bias)


def _post_kernel(o_ref, op_ref, oc_ref, ocp_ref, h_ref, hp_ref, p_ref, gm_ref, gf_ref, wout_ref,
                 gffn_ref, wup_ref, cw_ref, cb_ref, wd_ref, gple_ref, wpg_ref, wple_ref, fg_ref,
                 out_ref, h1_sc, m_sc, act_sc, *, ts, final):
    half = ATT_W // 2

    def mix(o, oc, h):
        mixed = jnp.concatenate([_rms(o[:, :half], gm_ref[...]).astype(BF16), oc,
                                 _rms(o[:, half:], gf_ref[...]).astype(BF16)], axis=-1)
        return h + _dot(mixed, wout_ref[...])

    keep = (pl.program_id(1) > 0).astype(F32)
    h1_prev = mix(op_ref[0], ocp_ref[0], hp_ref[0]) * keep
    h1 = mix(o_ref[0], oc_ref[0], h_ref[0])
    h1_sc[...] = h1
    m_sc[0:HALO, :] = _rms(h1_prev, gffn_ref[...]).astype(BF16)
    m_sc[HALO:HALO + ts, :] = _rms(h1, gffn_ref[...]).astype(BF16)

    def up_conv(lo, width):
        u = _dot(m_sc[...], wup_ref[:, lo:lo + width])
        return (cw_ref[0:1, lo:lo + width] * u[HALO - 2:HALO - 2 + ts, :]
                + cw_ref[1:2, lo:lo + width] * u[HALO - 1:HALO - 1 + ts, :]
                + cw_ref[2:3, lo:lo + width] * u[HALO:HALO + ts, :] + cb_ref[:, lo:lo + width])

    for lo, width in FF_CHUNKS:
        gate = up_conv(lo, width)
        val = up_conv(D_FF + lo, width)
        act_sc[:, lo:lo + width] = (gate * jax.nn.sigmoid(gate) * val).astype(BF16)
    h2 = h1_sc[...] + _dot(act_sc[...], wd_ref[...])

    gate = jax.nn.sigmoid(_dot(_rms(h2, gple_ref[...]).astype(BF16), wpg_ref[...]))
    out = h2 + gate * _dot(p_ref[0].astype(BF16), wple_ref[...])
    if final:
        out = _rms(out, fg_ref[...])
    out_ref[0] = out


def _post_call(o, oc, h, p, layer, w, *, final, ts=TS_POST):
    b, s, _ = h.shape
    halo_blocks = ts // HALO
    tile = lambda width: pl.BlockSpec((1, ts, width), lambda bi, si: (bi, si, 0))
    halo = lambda width: pl.BlockSpec(
        (1, HALO, width), lambda bi, si: (bi, jnp.maximum(si * halo_blocks - 1, 0), 0))

    def resident(*shape):
        return pl.BlockSpec((None,) + shape, lambda bi, si: (layer,) + (0,) * len(shape),
                            pipeline_mode=pl.Buffered(1))

    return pl.pallas_call(
        functools.partial(_post_kernel, ts=ts, final=final),
        grid=(b, s // ts),
        in_specs=[
            tile(ATT_W), halo(ATT_W), tile(CONV_WIDTH), halo(CONV_WIDTH), tile(D_MODEL),
            halo(D_MODEL),
            pl.BlockSpec((None, 1, ts, PLE_DIM), lambda bi, si: (layer, bi, si, 0)),
            _layer_spec(layer, 1, ATT_W // 2), _layer_spec(layer, 1, ATT_W // 2),
            resident(D_MODEL, D_MODEL),
            _layer_spec(layer, 1, D_MODEL), resident(D_MODEL, 2 * D_FF),
            _layer_spec(layer, CONV_K, 2 * D_FF), _layer_spec(layer, 1, 2 * D_FF),
            resident(D_FF, D_MODEL),
            _layer_spec(layer, 1, D_MODEL), resident(D_MODEL, D_MODEL), resident(PLE_DIM, D_MODEL),
            _const_spec(1, D_MODEL),
        ],
        out_specs=tile(D_MODEL),
        out_shape=jax.ShapeDtypeStruct(h.shape, F32),
        scratch_shapes=[pltpu.VMEM((ts, D_MODEL), F32), pltpu.VMEM((ts + HALO, D_MODEL), BF16),
                        pltpu.VMEM((ts, D_FF), BF16)],
        compiler_params=pltpu.CompilerParams(
            dimension_semantics=("arbitrary", "arbitrary"), vmem_limit_bytes=VMEM_LIMIT),
        name="post_attention",
    )(o, o, oc, oc, h, h, p, w["mla_out_norm"], w["fox_out_norm"], w["w_out"], w["ffn_norm"],
      w["w_up"], w["ffn_conv_w"], w["ffn_conv_b"], w["w_down"], w["ple_norm"], w["w_ple_gate"],
      w["w_ple"], w["final_norm"])


def _extras_base(h):
    return h * LANES + (HALF if h % 2 == 0 else 0)


def _placement():
    m = np.zeros((LANES, 2 * HW), np.float32)
    for t in range(3):
        for h in range(FOX_HEADS):
            m[FOX_HEADS * t + h, _extras_base(h) + t] = 1.0
            m[FOX_HEADS * t + h, HW + _extras_base(h) + 3 + t] = -1.0
    return jnp.asarray(m, BF16)


def _extra_ones():
    row = np.zeros((1, 2 * HW), np.float32)
    for h in range(FOX_HEADS):
        row[0, _extras_base(h) + 3:_extras_base(h) + 6] = 1.0
        row[0, HW + _extras_base(h):HW + _extras_base(h) + 3] = 1.0
    return jnp.asarray(row)


def _v_ones():
    col = np.zeros((HW, LANES), np.float32)
    for h in range(MLA_HEADS):
        col[_extras_base(h)] = 1.0
    return jnp.asarray(col)


def _pad_last(x, before, width):
    return jnp.pad(x, [(0, 0)] * (x.ndim - 1) + [(before, width - before - x.shape[-1])])


def _prepare_weights(attn_norm, w_in, b_forget, q_norm, w_uq, kv_norm, w_ukv, conv_w,
                     mla_out_norm, conv_out_norm, fox_out_norm, w_out, ffn_norm, w_up,
                     ffn_conv_w, ffn_conv_b, w_down, ple_norm, w_ple_gate, w_ple, final_norm):
    depth = w_in.shape[0]
    half = MLA_ROPE // 2
    wr = w_in[..., O_ZR:O_ZB]
    wr_swapped = jnp.concatenate([wr[..., half:], wr[..., :half]], axis=-1)
    w_in_p = jnp.concatenate([
        w_in[..., O_ZQ:O_ZR],
        _pad_last(wr, MLA_NOPE, LANES), _pad_last(wr_swapped, MLA_NOPE, LANES),
        w_in[..., O_ZB:O_FV],
        _pad_last(w_in[..., O_FF:O_END], 0, LANES)], axis=-1).astype(BF16)

    uq = w_uq.reshape(depth, Q_RANK, MLA_HEADS, MLA_NOPE + MLA_ROPE)
    uq_rot = jnp.concatenate([uq[..., MLA_NOPE + half:], uq[..., MLA_NOPE:MLA_NOPE + half]], axis=-1)
    w_uq_p = jnp.concatenate([
        _pad_last(uq, 0, LANES).reshape(depth, Q_RANK, HW),
        _pad_last(uq_rot, MLA_NOPE, LANES).reshape(depth, Q_RANK, HW)], axis=-1).astype(BF16)
    ukv = w_ukv.reshape(depth, KV_RANK, MLA_HEADS, MLA_NOPE + MLA_V)
    w_uk = _pad_last(ukv[..., :MLA_NOPE], 0, LANES).reshape(depth, KV_RANK, HW).astype(BF16)
    w_uv = ukv[..., MLA_NOPE:].reshape(depth, KV_RANK, MLA_HEADS * MLA_V)

    rows = lambda v: v.reshape(depth, 1, -1)
    return {
        "attn_norm": rows(attn_norm),
        "w_in": w_in_p,
        "q_norm": rows(q_norm),
        "w_uq": w_uq_p,
        "kv_norm": rows(kv_norm),
        "w_uk": w_uk,
        "w_uv_t": jnp.swapaxes(w_uv, 1, 2).astype(BF16),
        "w_fv_t": jnp.swapaxes(w_in[..., O_FV:O_FF], 1, 2).astype(BF16),
        "v_ones": _v_ones(),
        "conv_w": conv_w,
        "conv_out_norm": rows(conv_out_norm),
        "b_forget": rows(_pad_last(b_forget, 0, LANES)),
        "place": _placement(),
        "extra_ones": _extra_ones(),
        "mla_out_norm": rows(mla_out_norm),
        "fox_out_norm": rows(fox_out_norm),
        "w_out": w_out.astype(BF16),
        "ffn_norm": rows(ffn_norm),
        "w_up": w_up.astype(BF16),
        "ffn_conv_w": ffn_conv_w,
        "ffn_conv_b": rows(ffn_conv_b),
        "w_down": w_down.astype(BF16),
        "ple_norm": rows(ple_norm),
        "w_ple_gate": w_ple_gate.astype(BF16),
        "w_ple": w_ple.astype(BF16),
        "final_norm": final_norm.reshape(1, D_MODEL),
    }


def _rope_rows(positions):
    b, s = positions.shape
    n_freq = MLA_ROPE // 2
    inv_freq = ROPE_THETA ** (-jnp.arange(0, MLA_ROPE, 2, dtype=F32) / MLA_ROPE)
    ang = positions.astype(F32)[..., None] * inv_freq
    dense = lax.optimization_barrier(ang.reshape(b, s * n_freq // LANES, LANES))
    cos, sin = lax.optimization_barrier((jnp.cos(dense), jnp.sin(dense)))
    cos = cos.reshape(b, s, n_freq)
    sin = sin.reshape(b, s, n_freq)
    lead = (b, s, MLA_NOPE)
    tail = (b, s, LANES - MLA_NOPE - MLA_ROPE)
    cos_t = jnp.concatenate([jnp.ones(lead, F32), cos, cos, jnp.zeros(tail, F32)], axis=-1)
    sin_t = jnp.concatenate([jnp.zeros(lead, F32), -sin, sin, jnp.zeros(tail, F32)], axis=-1)
    return cos_t, sin_t


def kernel(x, p, positions, attn_norm, w_in, b_forget, q_norm, w_uq, kv_norm, w_ukv, conv_w,
           mla_out_norm, conv_out_norm, fox_out_norm, w_out, ffn_norm, w_up, ffn_conv_w,
           ffn_conv_b, w_down, ple_norm, w_ple_gate, w_ple, final_norm):
    depth = w_in.shape[0]
    cos_t, sin_t = _rope_rows(positions)
    w = _prepare_weights(attn_norm, w_in, b_forget, q_norm, w_uq, kv_norm, w_ukv, conv_w,
                         mla_out_norm, conv_out_norm, fox_out_norm, w_out, ffn_norm, w_up,
                         ffn_conv_w, ffn_conv_b, w_down, ple_norm, w_ple_gate, w_ple, final_norm)
    h = x
    for layer in range(depth):
        q, k, vt, oc = _in_call(h, cos_t, sin_t, layer, w)
        o = _attn_call(q, k, vt)
        h = _post_call(o, oc, h, p, layer, w, final=(layer == depth - 1))
    return h
```

```python
import functools
import math

import jax
import jax.numpy as jnp
import numpy as np
from jax import lax
from jax.experimental import pallas as pl
from jax.experimental.pallas import tpu as pltpu

F32 = jnp.float32
BF16 = jnp.bfloat16

D_MODEL = 1024
MLA_HEADS = 6
MLA_NOPE = 64
MLA_ROPE = 32
MLA_V = 64
Q_RANK = 256
KV_RANK = 128
FOX_HEADS = 6
FOX_DIM = 64
FOX_W = FOX_HEADS * FOX_DIM
CONV_WIDTH = 256
CONV_K = 3
D_FF = 2816
PLE_DIM = 256
ROPE_THETA = 10000.0
EPS = 1e-6
LOG2E = math.log2(math.e)

IN_SIZES = (Q_RANK, KV_RANK, MLA_ROPE, CONV_WIDTH, CONV_WIDTH, CONV_WIDTH, FOX_W, FOX_W, FOX_W,
            FOX_HEADS)
O_ZQ, O_ZKV, O_ZR, O_ZB, O_ZC, O_ZH, O_FQ, O_FK, O_FV, O_FF, O_END = (
    sum(IN_SIZES[:n]) for n in range(len(IN_SIZES) + 1))

LANES = 128
HALF = LANES // 2
BF16_SUBLANES = 16
V7X_VMEM_BYTES = 64 * 1024 * 1024
VMEM_LIMIT = V7X_VMEM_BYTES - 8 * 1024 * 1024

HEADS = MLA_HEADS + FOX_HEADS
HW = MLA_HEADS * LANES
ATT_W = HEADS * 64

C_ZQ = 0
C_ZKV = C_ZQ + Q_RANK
C_ZR = C_ZKV + KV_RANK
C_ZRS = C_ZR + LANES
C_ZB = C_ZRS + LANES
C_ZC = C_ZB + CONV_WIDTH
C_ZH = C_ZC + CONV_WIDTH
C_FQ = C_ZH + CONV_WIDTH
C_FK = C_FQ + FOX_W
C_FF = C_FK + FOX_W
N_Z = C_FF + LANES

TS_IN = 512
T_ATT = 512
Q_SPLIT = 2
UNROLL_BELOW = 14
UNROLL_DIAG = 8
LEAD = 1
MASK_BIAS = -1e30
TS_POST = 512
HALO = BF16_SUBLANES
FF_CHUNKS = ((0, 768), (768, 768), (1536, 768), (2304, 512))


def _rms(x, g):
    return x * lax.rsqrt(jnp.mean(x * x, axis=-1, keepdims=True) + EPS) * g


def _split3(x):
    hi = x.astype(BF16)
    r = x - hi.astype(F32)
    mid = r.astype(BF16)
    lo = (r - mid.astype(F32)).astype(BF16)
    return hi, mid, lo


def _dot(a, b):
    return jnp.dot(a, b, preferred_element_type=F32)


def _dot_nt(a, b):
    return lax.dot_general(a, b, (((1,), (1,)), ((), ())), preferred_element_type=F32)


def _layer_spec(layer, *shape):
    return pl.BlockSpec((None,) + shape, lambda *_: (layer,) + (0,) * len(shape))


def _const_spec(*shape):
    return pl.BlockSpec(shape, lambda *_: (0,) * len(shape))


def _in_kernel(x_ref, g_ref, win_ref, qn_ref, wuq_ref, kvn_ref, wuk_ref, wuvt_ref, wfvt_ref,
               cos_ref, sin_ref, cw_ref, cn_ref, bf_ref, place_ref, ones_ref, vones_ref,
               q_ref, k_ref, vt_ref, oc_ref, xs_sc, cum_sc, *, ts):
    @pl.when(pl.program_id(1) == 0)
    def _():
        xs_sc[0:8, :] = jnp.zeros((8, CONV_WIDTH), F32)
        cum_sc[...] = jnp.zeros_like(cum_sc)

    a = _rms(x_ref[0], g_ref[...]).astype(BF16)
    z = _dot(a, win_ref[...])

    def proj(lo, width):
        return z[:, lo:lo + width]

    cos_t = cos_ref[0]
    sin_t = sin_ref[0]
    low_lanes = lax.broadcasted_iota(jnp.int32, (ts, LANES), 1) < HALF
    top_rows = lax.broadcasted_iota(jnp.int32, (LANES, ts), 0) < HALF

    def head_blocks(h):
        return (slice(h * LANES, (h + 1) * LANES),
                slice((h // 2) * LANES, (h // 2 + 1) * LANES), h % 2 == 0)

    def v_row(pair_t, h):
        own, pair, even = head_blocks(h)
        mine = top_rows if even else jnp.logical_not(top_rows)
        ones = jnp.tile(vones_ref[own, :], (1, ts // LANES))
        return jnp.where(mine, pair_t[pair, :], ones).astype(BF16)

    qn = _rms(proj(C_ZQ, Q_RANK), qn_ref[...]).astype(BF16)
    q_main = _dot(qn, wuq_ref[:, 0:HW])
    q_swap = _dot(qn, wuq_ref[:, HW:2 * HW])
    mla_scale = LOG2E * (MLA_NOPE + MLA_ROPE) ** -0.5
    kvn = _rms(proj(C_ZKV, KV_RANK), kvn_ref[...]).astype(BF16)
    k_nope = _dot(kvn, wuk_ref[...])
    v_mla_t = _dot_nt(wuvt_ref[...], kvn)
    k_rope = proj(C_ZR, LANES) * cos_t + proj(C_ZRS, LANES) * sin_t
    for h in range(MLA_HEADS):
        own = head_blocks(h)[0]
        q_ref[0, h] = ((q_main[:, own] * cos_t + q_swap[:, own] * sin_t) * mla_scale).astype(BF16)
        k_ref[0, h] = (k_nope[:, own] + k_rope).astype(BF16)
        vt_ref[0, h, 0] = v_row(v_mla_t, h)

    xc = proj(C_ZC, CONV_WIDTH) * proj(C_ZH, CONV_WIDTH)
    xs_sc[8:8 + ts, :] = xc
    y = (cw_ref[0:1, :] * xs_sc[6:6 + ts, :] + cw_ref[1:2, :] * xs_sc[7:7 + ts, :]
         + cw_ref[2:3, :] * xc)
    oc_ref[0] = _rms(proj(C_ZB, CONV_WIDTH) * y, cn_ref[...]).astype(BF16)
    xs_sc[0:8, :] = xs_sc[ts:ts + 8, :]

    ff = proj(C_FF, LANES) + bf_ref[...]
    log_f = jnp.minimum(ff, 0.0) - jnp.log1p(jnp.exp(-jnp.abs(ff)))
    row = lax.broadcasted_iota(jnp.int32, (ts, ts), 0)
    col = lax.broadcasted_iota(jnp.int32, (ts, ts), 1)
    tri = (col <= row).astype(BF16)
    f_hi, f_mid, f_lo = _split3(log_f)
    cum = _dot(tri, f_hi) + _dot(tri, f_mid) + _dot(tri, f_lo) + cum_sc[0:1, :]
    cum_sc[0:1, :] = cum[ts - 1:ts, :]
    c_hi, c_mid, c_lo = (term.astype(F32) for term in _split3(cum * LOG2E))
    lane = lax.broadcasted_iota(jnp.int32, (ts, LANES), 1)
    nh = FOX_HEADS
    terms = jnp.where(lane < nh, c_hi,
                      jnp.where(lane < 2 * nh, pltpu.roll(c_mid, nh, axis=1),
                                jnp.where(lane < 3 * nh, pltpu.roll(c_lo, 2 * nh, axis=1), 0.0)))
    extras = _dot(terms.astype(BF16), place_ref[...]) + ones_ref[...]
    extra_q = extras[:, :HW]
    extra_k = extras[:, HW:]

    fq = proj(C_FQ, FOX_W) * (LOG2E * FOX_DIM ** -0.5)
    fk = proj(C_FK, FOX_W)
    fv_t = _dot_nt(wfvt_ref[...], a)
    for h in range(FOX_HEADS):
        own, pair, even = head_blocks(h)
        mine = low_lanes if even else jnp.logical_not(low_lanes)
        q_ref[0, MLA_HEADS + h] = jnp.where(mine, fq[:, pair], extra_q[:, own]).astype(BF16)
        k_ref[0, MLA_HEADS + h] = jnp.where(mine, fk[:, pair], extra_k[:, own]).astype(BF16)
        vt_ref[0, MLA_HEADS + h, 0] = v_row(fv_t, h)


def _in_call(h, cos_t, sin_t, layer, w, *, ts=TS_IN):
    b, s, _ = h.shape
    assert ts == T_ATT
    tile = lambda width: pl.BlockSpec((1, ts, width), lambda bi, si: (bi, si, 0))
    head_spec = pl.BlockSpec((1, HEADS, ts, LANES), lambda bi, si: (bi, 0, si, 0))
    head_shape = jax.ShapeDtypeStruct((b, HEADS, s, LANES), BF16)
    vt_spec = pl.BlockSpec((1, HEADS, 1, LANES, ts), lambda bi, si: (bi, 0, si, 0, 0))
    vt_shape = jax.ShapeDtypeStruct((b, HEADS, s // ts, LANES, ts), BF16)
    return pl.pallas_call(
        functools.partial(_in_kernel, ts=ts),
        grid=(b, s // ts),
        in_specs=[
            tile(D_MODEL),
            _layer_spec(layer, 1, D_MODEL),
            _layer_spec(layer, D_MODEL, N_Z),
            _layer_spec(layer, 1, Q_RANK),
            _layer_spec(layer, Q_RANK, 2 * HW),
            _layer_spec(layer, 1, KV_RANK),
            _layer_spec(layer, KV_RANK, HW),
            _layer_spec(layer, MLA_HEADS * MLA_V, KV_RANK),
            _layer_spec(layer, FOX_W, D_MODEL),
            tile(LANES),
            tile(LANES),
            _layer_spec(layer, CONV_K, CONV_WIDTH),
            _layer_spec(layer, 1, CONV_WIDTH),
            _layer_spec(layer, 1, LANES),
            _const_spec(LANES, 2 * HW),
            _const_spec(1, 2 * HW),
            _const_spec(HW, LANES),
        ],
        out_specs=[head_spec, head_spec, vt_spec, tile(CONV_WIDTH)],
        out_shape=[head_shape, head_shape, vt_shape,
                   jax.ShapeDtypeStruct((b, s, CONV_WIDTH), BF16)],
        scratch_shapes=[pltpu.VMEM((ts + 8, CONV_WIDTH), F32), pltpu.VMEM((8, LANES), F32)],
        compiler_params=pltpu.CompilerParams(
            dimension_semantics=("arbitrary", "arbitrary"), vmem_limit_bytes=VMEM_LIMIT),
        name="in_proj",
    )(h, w["attn_norm"], w["w_in"], w["q_norm"], w["w_uq"], w["kv_norm"], w["w_uk"], w["w_uv_t"],
      w["w_fv_t"], cos_t, sin_t, w["conv_w"], w["conv_out_norm"], w["b_forget"], w["place"],
      w["extra_ones"], w["v_ones"])


def _attn_kernel(q_ref, k_ref, vt_ref, bias_ref, o_ref, s_sc, m_sc, acc_sc, *, t, nb):
    m_sc[...] = jnp.full_like(m_sc, -jnp.inf)
    acc_sc[...] = jnp.zeros_like(acc_sc)
    row = lax.broadcasted_iota(jnp.int32, (LANES, t), 0)

    tq = t // Q_SPLIT

    def scores(hh, i, j, slot, diag, part):
        cols = slice(part * tq, (part + 1) * tq)
        k = k_ref[0, hh, pl.ds(pl.multiple_of(j * t, t), t), :]
        q = q_ref[0, hh, pl.ds(pl.multiple_of(i * t + part * tq, tq), tq), :]
        s_t = _dot_nt(k, q)
        s_sc[slot, hh, :, cols] = s_t + bias_ref[:, cols] if diag else s_t

    def consume(hh, i, j, slot, part):
        cols = slice(part * tq, (part + 1) * tq)
        m_prev = m_sc[hh, i, :, cols]
        m_new = jnp.maximum(m_prev, jnp.max(s_sc[slot, hh, :, cols], axis=0, keepdims=True))
        alpha = jnp.exp2(m_prev - m_new)
        p_t = jnp.exp2(s_sc[slot, hh, :, cols] - m_new).astype(BF16)
        acc = alpha * acc_sc[hh, i, :, cols] + _dot(vt_ref[0, hh, j], p_t)
        acc_sc[hh, i, :, cols] = acc
        m_sc[hh, i, :, cols] = m_new
        return acc

    def staggered(i, j, ni, nj, slot, diag):
        pieces = [(hh, part) for hh in range(2) for part in range(Q_SPLIT)]
        for hh, part in pieces[:LEAD]:
            scores(hh, ni, nj, 1 - slot, diag, part)
        accs = [[], []]
        for n, (hh, part) in enumerate(pieces):
            if n + LEAD < len(pieces):
                scores(pieces[n + LEAD][0], ni, nj, 1 - slot, diag, pieces[n + LEAD][1])
            accs[hh].append(consume(hh, i, j, slot, part))
        return [jnp.concatenate(parts, axis=1) for parts in accs]

    def below(i, j, slot):
        wrap = j + 1 == i
        ni = jnp.where(wrap, jnp.minimum(i + 1, nb - 1), i)
        nj = jnp.where(wrap, 0, j + 1)
        staggered(i, j, ni, nj, slot, False)
        return ni, nj

    first = (jnp.int32(1), jnp.int32(0))
    for hh in range(2):
        for part in range(Q_SPLIT):
            scores(hh, *first, 0, False, part)
    n_below = nb * (nb - 1) // 2
    assert n_below % UNROLL_BELOW == 0 and nb % UNROLL_DIAG == 0
    assert UNROLL_BELOW % 2 == 0 and UNROLL_DIAG % 2 == 0

    def below_body(_, unit):
        for u in range(UNROLL_BELOW):
            unit = below(*unit, u % 2)
        return unit

    lax.fori_loop(0, n_below // UNROLL_BELOW, below_body, first)

    def diagonal(i, slot):
        ni = jnp.minimum(i + 1, nb - 1)
        acc0, acc1 = staggered(i, i, ni, ni, slot, True)
        o_t = jnp.where(row < HALF, acc0 / acc0[HALF:HALF + 1, :], acc1 / acc1[0:1, :])
        o_ref[0, pl.ds(pl.multiple_of(i * t, t), t), :] = o_t.T
        return ni

    zero = jnp.int32(0)
    for hh in range(2):
        for part in range(Q_SPLIT):
            scores(hh, zero, zero, 0, True, part)

    def diagonal_body(_, i):
        for u in range(UNROLL_DIAG):
            i = diagonal(i, u % 2)
        return i

    lax.fori_loop(0, nb // UNROLL_DIAG, diagonal_body, zero)


def _attn_call(q, k, vt, *, t=T_ATT):
    b, _, s, _ = q.shape
    nb = s // t
    key = lax.broadcasted_iota(jnp.int32, (t, t), 0)
    query = lax.broadcasted_iota(jnp.int32, (t, t), 1)
    bias = jnp.where(key <= query, 0.0, MASK_BIAS).astype(F32)
    seq = pl.BlockSpec((1, 2, s, LANES), lambda bi, g: (bi, g, 0, 0))
    return pl.pallas_call(
        functools.partial(_attn_kernel, t=t, nb=nb),
        grid=(b, HEADS // 2),
        in_specs=[seq, seq,
                  pl.BlockSpec((1, 2, nb, LANES, t), lambda bi, g: (bi, g, 0, 0, 0)),
                  _const_spec(t, t)],
        out_specs=pl.BlockSpec((1, s, LANES), lambda bi, g: (bi, 0, g)),
        out_shape=jax.ShapeDtypeStruct((b, s, ATT_W), F32),
        scratch_shapes=[pltpu.VMEM((2, 2, t, t), F32), pltpu.VMEM((2, nb, 1, t), F32),
                        pltpu.VMEM((2, nb, LANES, t), F32)],
        compiler_params=pltpu.CompilerParams(
            dimension_semantics=("arbitrary", "arbitrary"), vmem_limit_bytes=VMEM_LIMIT),
        name="attention",
    )(q, k, vt, bias)


def _post_kernel(o_ref, op_ref, oc_ref, ocp_ref, h_ref, hp_ref, p_ref, gm_ref, gf_ref, wout_ref,
                 gffn_ref, wup_ref, cw_ref, cb_ref, wd_ref, gple_ref, wpg_ref, wple_ref, fg_ref,
                 out_ref, h1_sc, m_sc, act_sc, *, ts, final):
    half = ATT_W // 2

    def mix(o, oc, h):
        mixed = jnp.concatenate([_rms(o[:, :half], gm_ref[...]).astype(BF16), oc,
                                 _rms(o[:, half:], gf_ref[...]).astype(BF16)], axis=-1)
        return h + _dot(mixed, wout_ref[...])

    keep = (pl.program_id(1) > 0).astype(F32)
    h1_prev = mix(op_ref[0], ocp_ref[0], hp_ref[0]) * keep
    h1 = mix(o_ref[0], oc_ref[0], h_ref[0])
    h1_sc[...] = h1
    m_sc[0:HALO, :] = _rms(h1_prev, gffn_ref[...]).astype(BF16)
    m_sc[HALO:HALO + ts, :] = _rms(h1, gffn_ref[...]).astype(BF16)

    def up_conv(lo, width):
        u = _dot(m_sc[...], wup_ref[:, lo:lo + width])
        return (cw_ref[0:1, lo:lo + width] * u[HALO - 2:HALO - 2 + ts, :]
                + cw_ref[1:2, lo:lo + width] * u[HALO - 1:HALO - 1 + ts, :]
                + cw_ref[2:3, lo:lo + width] * u[HALO:HALO + ts, :] + cb_ref[:, lo:lo + width])

    for lo, width in FF_CHUNKS:
        gate = up_conv(lo, width)
        val = up_conv(D_FF + lo, width)
        act_sc[:, lo:lo + width] = (gate * jax.nn.sigmoid(gate) * val).astype(BF16)
    h2 = h1_sc[...] + _dot(act_sc[...], wd_ref[...])

    gate = jax.nn.sigmoid(_dot(_rms(h2, gple_ref[...]).astype(BF16), wpg_ref[...]))
    out = h2 + gate * _dot(p_ref[0].astype(BF16), wple_ref[...])
    if final:
        out = _rms(out, fg_ref[...])
    out_ref[0] = out


def _post_call(o, oc, h, p, layer, w, *, final, ts=TS_POST):
    b, s, _ = h.shape
    halo_blocks = ts // HALO
    tile = lambda width: pl.BlockSpec((1, ts, width), lambda bi, si: (bi, si, 0))
    halo = lambda width: pl.BlockSpec(
        (1, HALO, width), lambda bi, si: (bi, jnp.maximum(si * halo_blocks - 1, 0), 0))

    def resident(*shape):
        return pl.BlockSpec((None,) + shape, lambda bi, si: (layer,) + (0,) * len(shape),
                            pipeline_mode=pl.Buffered(1))

    return pl.pallas_call(
        functools.partial(_post_kernel, ts=ts, final=final),
        grid=(b, s // ts),
        in_specs=[
            tile(ATT_W), halo(ATT_W), tile(CONV_WIDTH), halo(CONV_WIDTH), tile(D_MODEL),
            halo(D_MODEL),
            pl.BlockSpec((None, 1, ts, PLE_DIM), lambda bi, si: (layer, bi, si, 0)),
            _layer_spec(layer, 1, ATT_W // 2), _layer_spec(layer, 1, ATT_W // 2),
            resident(D_MODEL, D_MODEL),
            _layer_spec(layer, 1, D_MODEL), resident(D_MODEL, 2 * D_FF),
            _layer_spec(layer, CONV_K, 2 * D_FF), _layer_spec(layer, 1, 2 * D_FF),
            resident(D_FF, D_MODEL),
            _layer_spec(layer, 1, D_MODEL), resident(D_MODEL, D_MODEL), resident(PLE_DIM, D_MODEL),
            _const_spec(1, D_MODEL),
        ],
        out_specs=tile(D_MODEL),
        out_shape=jax.ShapeDtypeStruct(h.shape, F32),
        scratch_shapes=[pltpu.VMEM((ts, D_MODEL), F32), pltpu.VMEM((ts + HALO, D_MODEL), BF16),
                        pltpu.VMEM((ts, D_FF), BF16)],
        compiler_params=pltpu.CompilerParams(
            dimension_semantics=("arbitrary", "arbitrary"), vmem_limit_bytes=VMEM_LIMIT),
        name="post_attention",
    )(o, o, oc, oc, h, h, p, w["mla_out_norm"], w["fox_out_norm"], w["w_out"], w["ffn_norm"],
      w["w_up"], w["ffn_conv_w"], w["ffn_conv_b"], w["w_down"], w["ple_norm"], w["w_ple_gate"],
      w["w_ple"], w["final_norm"])


def _extras_base(h):
    return h * LANES + (HALF if h % 2 == 0 else 0)


def _placement():
    m = np.zeros((LANES, 2 * HW), np.float32)
    for t in range(3):
        for h in range(FOX_HEADS):
            m[FOX_HEADS * t + h, _extras_base(h) + t] = 1.0
            m[FOX_HEADS * t + h, HW + _extras_base(h) + 3 + t] = -1.0
    return jnp.asarray(m, BF16)


def _extra_ones():
    row = np.zeros((1, 2 * HW), np.float32)
    for h in range(FOX_HEADS):
        row[0, _extras_base(h) + 3:_extras_base(h) + 6] = 1.0
        row[0, HW + _extras_base(h):HW + _extras_base(h) + 3] = 1.0
    return jnp.asarray(row)


def _v_ones():
    col = np.zeros((HW, LANES), np.float32)
    for h in range(MLA_HEADS):
        col[_extras_base(h)] = 1.0
    return jnp.asarray(col)


def _pad_last(x, before, width):
    return jnp.pad(x, [(0, 0)] * (x.ndim - 1) + [(before, width - before - x.shape[-1])])


def _prepare_weights(attn_norm, w_in, b_forget, q_norm, w_uq, kv_norm, w_ukv, conv_w,
                     mla_out_norm, conv_out_norm, fox_out_norm, w_out, ffn_norm, w_up,
                     ffn_conv_w, ffn_conv_b, w_down, ple_norm, w_ple_gate, w_ple, final_norm):
    depth = w_in.shape[0]
    half = MLA_ROPE // 2
    wr = w_in[..., O_ZR:O_ZB]
    wr_swapped = jnp.concatenate([wr[..., half:], wr[..., :half]], axis=-1)
    w_in_p = jnp.concatenate([
        w_in[..., O_ZQ:O_ZR],
        _pad_last(wr, MLA_NOPE, LANES), _pad_last(wr_swapped, MLA_NOPE, LANES),
        w_in[..., O_ZB:O_FV],
        _pad_last(w_in[..., O_FF:O_END], 0, LANES)], axis=-1).astype(BF16)

    uq = w_uq.reshape(depth, Q_RANK, MLA_HEADS, MLA_NOPE + MLA_ROPE)
    uq_rot = jnp.concatenate([uq[..., MLA_NOPE + half:], uq[..., MLA_NOPE:MLA_NOPE + half]], axis=-1)
    w_uq_p = jnp.concatenate([
        _pad_last(uq, 0, LANES).reshape(depth, Q_RANK, HW),
        _pad_last(uq_rot, MLA_NOPE, LANES).reshape(depth, Q_RANK, HW)], axis=-1).astype(BF16)
    ukv = w_ukv.reshape(depth, KV_RANK, MLA_HEADS, MLA_NOPE + MLA_V)
    w_uk = _pad_last(ukv[..., :MLA_NOPE], 0, LANES).reshape(depth, KV_RANK, HW).astype(BF16)
    w_uv = ukv[..., MLA_NOPE:].reshape(depth, KV_RANK, MLA_HEADS * MLA_V)

    rows = lambda v: v.reshape(depth, 1, -1)
    return {
        "attn_norm": rows(attn_norm),
        "w_in": w_in_p,
        "q_norm": rows(q_norm),
        "w_uq": w_uq_p,
        "kv_norm": rows(kv_norm),
        "w_uk": w_uk,
        "w_uv_t": jnp.swapaxes(w_uv, 1, 2).astype(BF16),
        "w_fv_t": jnp.swapaxes(w_in[..., O_FV:O_FF], 1, 2).astype(BF16),
        "v_ones": _v_ones(),
        "conv_w": conv_w,
        "conv_out_norm": rows(conv_out_norm),
        "b_forget": rows(_pad_last(b_forget, 0, LANES)),
        "place": _placement(),
        "extra_ones": _extra_ones(),
        "mla_out_norm": rows(mla_out_norm),
        "fox_out_norm": rows(fox_out_norm),
        "w_out": w_out.astype(BF16),
        "ffn_norm": rows(ffn_norm),
        "w_up": w_up.astype(BF16),
        "ffn_conv_w": ffn_conv_w,
        "ffn_conv_b": rows(ffn_conv_b),
        "w_down": w_down.astype(BF16),
        "ple_norm": rows(ple_norm),
        "w_ple_gate": w_ple_gate.astype(BF16),
        "w_ple": w_ple.astype(BF16),
        "final_norm": final_norm.reshape(1, D_MODEL),
    }


def _rope_rows(positions):
    b, s = positions.shape
    n_freq = MLA_ROPE // 2
    inv_freq = ROPE_THETA ** (-jnp.arange(0, MLA_ROPE, 2, dtype=F32) / MLA_ROPE)
    ang = positions.astype(F32)[..., None] * inv_freq
    dense = lax.optimization_barrier(ang.reshape(b, s * n_freq // LANES, LANES))
    cos, sin = lax.optimization_barrier((jnp.cos(dense), jnp.sin(dense)))
    cos = cos.reshape(b, s, n_freq)
    sin = sin.reshape(b, s, n_freq)
    lead = (b, s, MLA_NOPE)
    tail = (b, s, LANES - MLA_NOPE - MLA_ROPE)
    cos_t = jnp.concatenate([jnp.ones(lead, F32), cos, cos, jnp.zeros(tail, F32)], axis=-1)
    sin_t = jnp.concatenate([jnp.zeros(lead, F32), -sin, sin, jnp.zeros(tail, F32)], axis=-1)
    return cos_t, sin_t


def kernel(x, p, positions, attn_norm, w_in, b_forget, q_norm, w_uq, kv_norm, w_ukv, conv_w,
           mla_out_norm, conv_out_norm, fox_out_norm, w_out, ffn_norm, w_up, ffn_conv_w,
           ffn_conv_b, w_down, ple_norm, w_ple_gate, w_ple, final_norm):
    depth = w_in.shape[0]
    cos_t, sin_t = _rope_rows(positions)
    w = _prepare_weights(attn_norm, w_in, b_forget, q_norm, w_uq, kv_norm, w_ukv, conv_w,
                         mla_out_norm, conv_out_norm, fox_out_norm, w_out, ffn_norm, w_up,
                         ffn_conv_w, ffn_conv_b, w_down, ple_norm, w_ple_gate, w_ple, final_norm)
    h = x
    for layer in range(depth):
        q, k, vt, oc = _in_call(h, cos_t, sin_t, layer, w)
        o = _attn_call(q, k, vt)
        h = _post_call(o, oc, h, p, layer, w, final=(layer == depth - 1))
    return h
```

```python
import functools
import math

import jax
import jax.numpy as jnp
import numpy as np
from jax import lax
from jax.experimental import pallas as pl
from jax.experimental.pallas import tpu as pltpu

F32 = jnp.float32
BF16 = jnp.bfloat16

D_MODEL = 1024
MLA_HEADS = 6
MLA_NOPE = 64
MLA_ROPE = 32
MLA_V = 64
Q_RANK = 256
KV_RANK = 128
FOX_HEADS = 6
FOX_DIM = 64
FOX_W = FOX_HEADS * FOX_DIM
CONV_WIDTH = 256
CONV_K = 3
D_FF = 2816
PLE_DIM = 256
ROPE_THETA = 10000.0
EPS = 1e-6
LOG2E = math.log2(math.e)

IN_SIZES = (Q_RANK, KV_RANK, MLA_ROPE, CONV_WIDTH, CONV_WIDTH, CONV_WIDTH, FOX_W, FOX_W, FOX_W,
            FOX_HEADS)
O_ZQ, O_ZKV, O_ZR, O_ZB, O_ZC, O_ZH, O_FQ, O_FK, O_FV, O_FF, O_END = (
    sum(IN_SIZES[:n]) for n in range(len(IN_SIZES) + 1))

LANES = 128
HALF = LANES // 2
BF16_SUBLANES = 16
V7X_VMEM_BYTES = 64 * 1024 * 1024
VMEM_LIMIT = V7X_VMEM_BYTES - 8 * 1024 * 1024

HEADS = MLA_HEADS + FOX_HEADS
HW = MLA_HEADS * LANES
ATT_W = HEADS * 64

C_ZQ = 0
C_ZKV = C_ZQ + Q_RANK
C_ZR = C_ZKV + KV_RANK
C_ZRS = C_ZR + LANES
C_ZB = C_ZRS + LANES
C_ZC = C_ZB + CONV_WIDTH
C_ZH = C_ZC + CONV_WIDTH
C_FQ = C_ZH + CONV_WIDTH
C_FK = C_FQ + FOX_W
C_FF = C_FK + FOX_W
N_Z = C_FF + LANES

TS_IN = 512
T_ATT = 512
Q_SPLIT = 2
UNROLL_BELOW = 14
UNROLL_DIAG = 8
LEAD = 1
MASK_BIAS = -1e30
TS_POST = 512
HALO = BF16_SUBLANES
FF_CHUNKS = ((0, 768), (768, 768), (1536, 768), (2304, 512))


def _rms(x, g):
    return x * lax.rsqrt(jnp.mean(x * x, axis=-1, keepdims=True) + EPS) * g


def _split3(x):
    hi = x.astype(BF16)
    r = x - hi.astype(F32)
    mid = r.astype(BF16)
    lo = (r - mid.astype(F32)).astype(BF16)
    return hi, mid, lo


def _dot(a, b):
    return jnp.dot(a, b, preferred_element_type=F32)


def _dot_nt(a, b):
    return lax.dot_general(a, b, (((1,), (1,)), ((), ())), preferred_element_type=F32)


def _layer_spec(layer, *shape):
    return pl.BlockSpec((None,) + shape, lambda *_: (layer,) + (0,) * len(shape))


def _const_spec(*shape):
    return pl.BlockSpec(shape, lambda *_: (0,) * len(shape))


def _in_kernel(x_ref, g_ref, win_ref, qn_ref, wuq_ref, kvn_ref, wuk_ref, wuvt_ref, wfvt_ref,
               cos_ref, sin_ref, cw_ref, cn_ref, bf_ref, place_ref, ones_ref, vones_ref,
               q_ref, k_ref, vt_ref, oc_ref, xs_sc, cum_sc, *, ts):
    @pl.when(pl.program_id(1) == 0)
    def _():
        xs_sc[0:8, :] = jnp.zeros((8, CONV_WIDTH), F32)
        cum_sc[...] = jnp.zeros_like(cum_sc)

    a = _rms(x_ref[0], g_ref[...]).astype(BF16)
    z = _dot(a, win_ref[...])

    def proj(lo, width):
        return z[:, lo:lo + width]

    cos_t = cos_ref[0]
    sin_t = sin_ref[0]
    low_lanes = lax.broadcasted_iota(jnp.int32, (ts, LANES), 1) < HALF
    top_rows = lax.broadcasted_iota(jnp.int32, (LANES, ts), 0) < HALF

    def head_blocks(h):
        return (slice(h * LANES, (h + 1) * LANES),
                slice((h // 2) * LANES, (h // 2 + 1) * LANES), h % 2 == 0)

    def v_row(pair_t, h):
        own, pair, even = head_blocks(h)
        mine = top_rows if even else jnp.logical_not(top_rows)
        ones = jnp.tile(vones_ref[own, :], (1, ts // LANES))
        return jnp.where(mine, pair_t[pair, :], ones).astype(BF16)

    qn = _rms(proj(C_ZQ, Q_RANK), qn_ref[...]).astype(BF16)
    q_main = _dot(qn, wuq_ref[:, 0:HW])
    q_swap = _dot(qn, wuq_ref[:, HW:2 * HW])
    mla_scale = LOG2E * (MLA_NOPE + MLA_ROPE) ** -0.5
    kvn = _rms(proj(C_ZKV, KV_RANK), kvn_ref[...]).astype(BF16)
    k_nope = _dot(kvn, wuk_ref[...])
    v_mla_t = _dot_nt(wuvt_ref[...], kvn)
    k_rope = proj(C_ZR, LANES) * cos_t + proj(C_ZRS, LANES) * sin_t
    for h in range(MLA_HEADS):
        own = head_blocks(h)[0]
        q_ref[0, h] = ((q_main[:, own] * cos_t + q_swap[:, own] * sin_t) * mla_scale).astype(BF16)
        k_ref[0, h] = (k_nope[:, own] + k_rope).astype(BF16)
        vt_ref[0, h, 0] = v_row(v_mla_t, h)

    xc = proj(C_ZC, CONV_WIDTH) * proj(C_ZH, CONV_WIDTH)
    xs_sc[8:8 + ts, :] = xc
    y = (cw_ref[0:1, :] * xs_sc[6:6 + ts, :] + cw_ref[1:2, :] * xs_sc[7:7 + ts, :]
         + cw_ref[2:3, :] * xc)
    oc_ref[0] = _rms(proj(C_ZB, CONV_WIDTH) * y, cn_ref[...]).astype(BF16)
    xs_sc[0:8, :] = xs_sc[ts:ts + 8, :]

    ff = proj(C_FF, LANES) + bf_ref[...]
    log_f = jnp.minimum(ff, 0.0) - jnp.log1p(jnp.exp(-jnp.abs(ff)))
    row = lax.broadcasted_iota(jnp.int32, (ts, ts), 0)
    col = lax.broadcasted_iota(jnp.int32, (ts, ts), 1)
    tri = (col <= row).astype(BF16)
    f_hi, f_mid, f_lo = _split3(log_f)
    cum = _dot(tri, f_hi) + _dot(tri, f_mid) + _dot(tri, f_lo) + cum_sc[0:1, :]
    cum_sc[0:1, :] = cum[ts - 1:ts, :]
    c_hi, c_mid, c_lo = (term.astype(F32) for term in _split3(cum * LOG2E))
    lane = lax.broadcasted_iota(jnp.int32, (ts, LANES), 1)
    nh = FOX_HEADS
    terms = jnp.where(lane < nh, c_hi,
                      jnp.where(lane < 2 * nh, pltpu.roll(c_mid, nh, axis=1),
                                jnp.where(lane < 3 * nh, pltpu.roll(c_lo, 2 * nh, axis=1), 0.0)))
    extras = _dot(terms.astype(BF16), place_ref[...]) + ones_ref[...]
    extra_q = extras[:, :HW]
    extra_k = extras[:, HW:]

    fq = proj(C_FQ, FOX_W) * (LOG2E * FOX_DIM ** -0.5)
    fk = proj(C_FK, FOX_W)
    fv_t = _dot_nt(wfvt_ref[...], a)
    for h in range(FOX_HEADS):
        own, pair, even = head_blocks(h)
        mine = low_lanes if even else jnp.logical_not(low_lanes)
        q_ref[0, MLA_HEADS + h] = jnp.where(mine, fq[:, pair], extra_q[:, own]).astype(BF16)
        k_ref[0, MLA_HEADS + h] = jnp.where(mine, fk[:, pair], extra_k[:, own]).astype(BF16)
        vt_ref[0, MLA_HEADS + h, 0] = v_row(fv_t, h)


def _in_call(h, cos_t, sin_t, layer, w, *, ts=TS_IN):
    b, s, _ = h.shape
    assert ts == T_ATT
    tile = lambda width: pl.BlockSpec((1, ts, width), lambda bi, si: (bi, si, 0))
    head_spec = pl.BlockSpec((1, HEADS, ts, LANES), lambda bi, si: (bi, 0, si, 0))
    head_shape = jax.ShapeDtypeStruct((b, HEADS, s, LANES), BF16)
    vt_spec = pl.BlockSpec((1, HEADS, 1, LANES, ts), lambda bi, si: (bi, 0, si, 0, 0))
    vt_shape = jax.ShapeDtypeStruct((b, HEADS, s // ts, LANES, ts), BF16)
    return pl.pallas_call(
        functools.partial(_in_kernel, ts=ts),
        grid=(b, s // ts),
        in_specs=[
            tile(D_MODEL),
            _layer_spec(layer, 1, D_MODEL),
            _layer_spec(layer, D_MODEL, N_Z),
            _layer_spec(layer, 1, Q_RANK),
            _layer_spec(layer, Q_RANK, 2 * HW),
            _layer_spec(layer, 1, KV_RANK),
            _layer_spec(layer, KV_RANK, HW),
            _layer_spec(layer, MLA_HEADS * MLA_V, KV_RANK),
            _layer_spec(layer, FOX_W, D_MODEL),
            tile(LANES),
            tile(LANES),
            _layer_spec(layer, CONV_K, CONV_WIDTH),
            _layer_spec(layer, 1, CONV_WIDTH),
            _layer_spec(layer, 1, LANES),
            _const_spec(LANES, 2 * HW),
            _const_spec(1, 2 * HW),
            _const_spec(HW, LANES),
        ],
        out_specs=[head_spec, head_spec, vt_spec, tile(CONV_WIDTH)],
        out_shape=[head_shape, head_shape, vt_shape,
                   jax.ShapeDtypeStruct((b, s, CONV_WIDTH), BF16)],
        scratch_shapes=[pltpu.VMEM((ts + 8, CONV_WIDTH), F32), pltpu.VMEM((8, LANES), F32)],
        compiler_params=pltpu.CompilerParams(
            dimension_semantics=("arbitrary", "arbitrary"), vmem_limit_bytes=VMEM_LIMIT),
        name="in_proj",
    )(h, w["attn_norm"], w["w_in"], w["q_norm"], w["w_uq"], w["kv_norm"], w["w_uk"], w["w_uv_t"],
      w["w_fv_t"], cos_t, sin_t, w["conv_w"], w["conv_out_norm"], w["b_forget"], w["place"],
      w["extra_ones"], w["v_ones"])


def _attn_kernel(q_ref, k_ref, vt_ref, bias_ref, o_ref, s_sc, m_sc, acc_sc, *, t, nb):
    m_sc[...] = jnp.full_like(m_sc, -jnp.inf)
    acc_sc[...] = jnp.zeros_like(acc_sc)
    row = lax.broadcasted_iota(jnp.int32, (LANES, t), 0)

    tq = t // Q_SPLIT

    def keys_needed(diag, part):
        return (part + 1) * tq if diag else t

    def scores(hh, i, j, slot, diag, part):
        cols = slice(part * tq, (part + 1) * tq)
        nk = keys_needed(diag, part)
        k = k_ref[0, hh, pl.ds(pl.multiple_of(j * t, t), nk), :]
        q = q_ref[0, hh, pl.ds(pl.multiple_of(i * t + part * tq, tq), tq), :]
        s_t = _dot_nt(k, q)
        s_sc[slot, hh, 0:nk, cols] = s_t + bias_ref[0:nk, cols] if diag else s_t

    def consume(hh, i, j, slot, diag, part):
        cols = slice(part * tq, (part + 1) * tq)
        nk = keys_needed(diag, part)
        m_prev = m_sc[hh, i, :, cols]
        m_new = jnp.maximum(m_prev, jnp.max(s_sc[slot, hh, 0:nk, cols], axis=0, keepdims=True))
        alpha = jnp.exp2(m_prev - m_new)
        p_t = jnp.exp2(s_sc[slot, hh, 0:nk, cols] - m_new).astype(BF16)
        acc = alpha * acc_sc[hh, i, :, cols] + _dot(vt_ref[0, hh, j, :, 0:nk], p_t)
        acc_sc[hh, i, :, cols] = acc
        m_sc[hh, i, :, cols] = m_new
        return acc

    def staggered(i, j, ni, nj, slot, diag):
        pieces = [(hh, part) for hh in range(2) for part in range(Q_SPLIT)]
        for hh, part in pieces[:LEAD]:
            scores(hh, ni, nj, 1 - slot, diag, part)
        accs = [[], []]
        for n, (hh, part) in enumerate(pieces):
            if n + LEAD < len(pieces):
                scores(pieces[n + LEAD][0], ni, nj, 1 - slot, diag, pieces[n + LEAD][1])
            accs[hh].append(consume(hh, i, j, slot, diag, part))
        return [jnp.concatenate(parts, axis=1) for parts in accs]

    def below(i, j, slot):
        wrap = j + 1 == i
        ni = jnp.where(wrap, jnp.minimum(i + 1, nb - 1), i)
        nj = jnp.where(wrap, 0, j + 1)
        staggered(i, j, ni, nj, slot, False)
        return ni, nj

    first = (jnp.int32(1), jnp.int32(0))
    for hh in range(2):
        for part in range(Q_SPLIT):
            scores(hh, *first, 0, False, part)
    n_below = nb * (nb - 1) // 2
    assert n_below % UNROLL_BELOW == 0 and nb % UNROLL_DIAG == 0
    assert UNROLL_BELOW % 2 == 0 and UNROLL_DIAG % 2 == 0

    def below_body(_, unit):
        for u in range(UNROLL_BELOW):
            unit = below(*unit, u % 2)
        return unit

    lax.fori_loop(0, n_below // UNROLL_BELOW, below_body, first)

    def diagonal(i, slot):
        ni = jnp.minimum(i + 1, nb - 1)
        acc0, acc1 = staggered(i, i, ni, ni, slot, True)
        o_t = jnp.where(row < HALF, acc0 / acc0[HALF:HALF + 1, :], acc1 / acc1[0:1, :])
        o_ref[0, pl.ds(pl.multiple_of(i * t, t), t), :] = o_t.T
        return ni

    zero = jnp.int32(0)
    for hh in range(2):
        for part in range(Q_SPLIT):
            scores(hh, zero, zero, 0, True, part)

    def diagonal_body(_, i):
        for u in range(UNROLL_DIAG):
            i = diagonal(i, u % 2)
        return i

    lax.fori_loop(0, nb // UNROLL_DIAG, diagonal_body, zero)


def _attn_call(q, k, vt, *, t=T_ATT):
    b, _, s, _ = q.shape
    nb = s // t
    key = lax.broadcasted_iota(jnp.int32, (t, t), 0)
    query = lax.broadcasted_iota(jnp.int32, (t, t), 1)
    bias = jnp.where(key <= query, 0.0, MASK_BIAS).astype(F32)
    seq = pl.BlockSpec((1, 2, s, LANES), lambda bi, g: (bi, g, 0, 0))
    return pl.pallas_call(
        functools.partial(_attn_kernel, t=t, nb=nb),
        grid=(b, HEADS // 2),
        in_specs=[seq, seq,
                  pl.BlockSpec((1, 2, nb, LANES, t), lambda bi, g: (bi, g, 0, 0, 0)),
                  _const_spec(t, t)],
        out_specs=pl.BlockSpec((1, s, LANES), lambda bi, g: (bi, 0, g)),
        out_shape=jax.ShapeDtypeStruct((b, s, ATT_W), F32),
        scratch_shapes=[pltpu.VMEM((2, 2, t, t), F32), pltpu.VMEM((2, nb, 1, t), F32),
                        pltpu.VMEM((2, nb, LANES, t), F32)],
        compiler_params=pltpu.CompilerParams(
            dimension_semantics=("arbitrary", "arbitrary"), vmem_limit_bytes=VMEM_LIMIT),
        name="attention",
    )(q, k, vt, bias)


def _post_kernel(o_ref, op_ref, oc_ref, ocp_ref, h_ref, hp_ref, p_ref, gm_ref, gf_ref, wout_ref,
                 gffn_ref, wup_ref, cw_ref, cb_ref, wd_ref, gple_ref, wpg_ref, wple_ref, fg_ref,
                 out_ref, h1_sc, m_sc, act_sc, *, ts, final):
    half = ATT_W // 2

    def mix(o, oc, h):
        mixed = jnp.concatenate([_rms(o[:, :half], gm_ref[...]).astype(BF16), oc,
                                 _rms(o[:, half:], gf_ref[...]).astype(BF16)], axis=-1)
        return h + _dot(mixed, wout_ref[...])

    keep = (pl.program_id(1) > 0).astype(F32)
    h1_prev = mix(op_ref[0], ocp_ref[0], hp_ref[0]) * keep
    h1 = mix(o_ref[0], oc_ref[0], h_ref[0])
    h1_sc[...] = h1
    m_sc[0:HALO, :] = _rms(h1_prev, gffn_ref[...]).astype(BF16)
    m_sc[HALO:HALO + ts, :] = _rms(h1, gffn_ref[...]).astype(BF16)

    def up_conv(lo, width):
        u = _dot(m_sc[...], wup_ref[:, lo:lo + width])
        return (cw_ref[0:1, lo:lo + width] * u[HALO - 2:HALO - 2 + ts, :]
                + cw_ref[1:2, lo:lo + width] * u[HALO - 1:HALO - 1 + ts, :]
                + cw_ref[2:3, lo:lo + width] * u[HALO:HALO + ts, :] + cb_ref[:, lo:lo + width])

    for lo, width in FF_CHUNKS:
        gate = up_conv(lo, width)
        val = up_conv(D_FF + lo, width)
        act_sc[:, lo:lo + width] = (gate * jax.nn.sigmoid(gate) * val).astype(BF16)
    h2 = h1_sc[...] + _dot(act_sc[...], wd_ref[...])

    gate = jax.nn.sigmoid(_dot(_rms(h2, gple_ref[...]).astype(BF16), wpg_ref[...]))
    out = h2 + gate * _dot(p_ref[0].astype(BF16), wple_ref[...])
    if final:
        out = _rms(out, fg_ref[...])
    out_ref[0] = out


def _post_call(o, oc, h, p, layer, w, *, final, ts=TS_POST):
    b, s, _ = h.shape
    halo_blocks = ts // HALO
    tile = lambda width: pl.BlockSpec((1, ts, width), lambda bi, si: (bi, si, 0))
    halo = lambda width: pl.BlockSpec(
        (1, HALO, width), lambda bi, si: (bi, jnp.maximum(si * halo_blocks - 1, 0), 0))

    def resident(*shape):
        return pl.BlockSpec((None,) + shape, lambda bi, si: (layer,) + (0,) * len(shape),
                            pipeline_mode=pl.Buffered(1))

    return pl.pallas_call(
        functools.partial(_post_kernel, ts=ts, final=final),
        grid=(b, s // ts),
        in_specs=[
            tile(ATT_W), halo(ATT_W), tile(CONV_WIDTH), halo(CONV_WIDTH), tile(D_MODEL),
            halo(D_MODEL),
            pl.BlockSpec((None, 1, ts, PLE_DIM), lambda bi, si: (layer, bi, si, 0)),
            _layer_spec(layer, 1, ATT_W // 2), _layer_spec(layer, 1, ATT_W // 2),
            resident(D_MODEL, D_MODEL),
            _layer_spec(layer, 1, D_MODEL), resident(D_MODEL, 2 * D_FF),
            _layer_spec(layer, CONV_K, 2 * D_FF), _layer_spec(layer, 1, 2 * D_FF),
            resident(D_FF, D_MODEL),
            _layer_spec(layer, 1, D_MODEL), resident(D_MODEL, D_MODEL), resident(PLE_DIM, D_MODEL),
            _const_spec(1, D_MODEL),
        ],
        out_specs=tile(D_MODEL),
        out_shape=jax.ShapeDtypeStruct(h.shape, F32),
        scratch_shapes=[pltpu.VMEM((ts, D_MODEL), F32), pltpu.VMEM((ts + HALO, D_MODEL), BF16),
                        pltpu.VMEM((ts, D_FF), BF16)],
        compiler_params=pltpu.CompilerParams(
            dimension_semantics=("arbitrary", "arbitrary"), vmem_limit_bytes=VMEM_LIMIT),
        name="post_attention",
    )(o, o, oc, oc, h, h, p, w["mla_out_norm"], w["fox_out_norm"], w["w_out"], w["ffn_norm"],
      w["w_up"], w["ffn_conv_w"], w["ffn_conv_b"], w["w_down"], w["ple_norm"], w["w_ple_gate"],
      w["w_ple"], w["final_norm"])


def _extras_base(h):
    return h * LANES + (HALF if h % 2 == 0 else 0)


def _placement():
    m = np.zeros((LANES, 2 * HW), np.float32)
    for t in range(3):
        for h in range(FOX_HEADS):
            m[FOX_HEADS * t + h, _extras_base(h) + t] = 1.0
            m[FOX_HEADS * t + h, HW + _extras_base(h) + 3 + t] = -1.0
    return jnp.asarray(m, BF16)


def _extra_ones():
    row = np.zeros((1, 2 * HW), np.float32)
    for h in range(FOX_HEADS):
        row[0, _extras_base(h) + 3:_extras_base(h) + 6] = 1.0
        row[0, HW + _extras_base(h):HW + _extras_base(h) + 3] = 1.0
    return jnp.asarray(row)


def _v_ones():
    col = np.zeros((HW, LANES), np.float32)
    for h in range(MLA_HEADS):
        col[_extras_base(h)] = 1.0
    return jnp.asarray(col)


def _pad_last(x, before, width):
    return jnp.pad(x, [(0, 0)] * (x.ndim - 1) + [(before, width - before - x.shape[-1])])


def _prepare_weights(attn_norm, w_in, b_forget, q_norm, w_uq, kv_norm, w_ukv, conv_w,
                     mla_out_norm, conv_out_norm, fox_out_norm, w_out, ffn_norm, w_up,
                     ffn_conv_w, ffn_conv_b, w_down, ple_norm, w_ple_gate, w_ple, final_norm):
    depth = w_in.shape[0]
    half = MLA_ROPE // 2
    wr = w_in[..., O_ZR:O_ZB]
    wr_swapped = jnp.concatenate([wr[..., half:], wr[..., :half]], axis=-1)
    w_in_p = jnp.concatenate([
        w_in[..., O_ZQ:O_ZR],
        _pad_last(wr, MLA_NOPE, LANES), _pad_last(wr_swapped, MLA_NOPE, LANES),
        w_in[..., O_ZB:O_FV],
        _pad_last(w_in[..., O_FF:O_END], 0, LANES)], axis=-1).astype(BF16)

    uq = w_uq.reshape(depth, Q_RANK, MLA_HEADS, MLA_NOPE + MLA_ROPE)
    uq_rot = jnp.concatenate([uq[..., MLA_NOPE + half:], uq[..., MLA_NOPE:MLA_NOPE + half]], axis=-1)
    w_uq_p = jnp.concatenate([
        _pad_last(uq, 0, LANES).reshape(depth, Q_RANK, HW),
        _pad_last(uq_rot, MLA_NOPE, LANES).reshape(depth, Q_RANK, HW)], axis=-1).astype(BF16)
    ukv = w_ukv.reshape(depth, KV_RANK, MLA_HEADS, MLA_NOPE + MLA_V)
    w_uk = _pad_last(ukv[..., :MLA_NOPE], 0, LANES).reshape(depth, KV_RANK, HW).astype(BF16)
    w_uv = ukv[..., MLA_NOPE:].reshape(depth, KV_RANK, MLA_HEADS * MLA_V)

    rows = lambda v: v.reshape(depth, 1, -1)
    return {
        "attn_norm": rows(attn_norm),
        "w_in": w_in_p,
        "q_norm": rows(q_norm),
        "w_uq": w_uq_p,
        "kv_norm": rows(kv_norm),
        "w_uk": w_uk,
        "w_uv_t": jnp.swapaxes(w_uv, 1, 2).astype(BF16),
        "w_fv_t": jnp.swapaxes(w_in[..., O_FV:O_FF], 1, 2).astype(BF16),
        "v_ones": _v_ones(),
        "conv_w": conv_w,
        "conv_out_norm": rows(conv_out_norm),
        "b_forget": rows(_pad_last(b_forget, 0, LANES)),
        "place": _placement(),
        "extra_ones": _extra_ones(),
        "mla_out_norm": rows(mla_out_norm),
        "fox_out_norm": rows(fox_out_norm),
        "w_out": w_out.astype(BF16),
        "ffn_norm": rows(ffn_norm),
        "w_up": w_up.astype(BF16),
        "ffn_conv_w": ffn_conv_w,
        "ffn_conv_b": rows(ffn_conv_b),
        "w_down": w_down.astype(BF16),
        "ple_norm": rows(ple_norm),
        "w_ple_gate": w_ple_gate.astype(BF16),
        "w_ple": w_ple.astype(BF16),
        "final_norm": final_norm.reshape(1, D_MODEL),
    }


def _rope_rows(positions):
    b, s = positions.shape
    n_freq = MLA_ROPE // 2
    inv_freq = ROPE_THETA ** (-jnp.arange(0, MLA_ROPE, 2, dtype=F32) / MLA_ROPE)
    ang = positions.astype(F32)[..., None] * inv_freq
    dense = lax.optimization_barrier(ang.reshape(b, s * n_freq // LANES, LANES))
    cos, sin = lax.optimization_barrier((jnp.cos(dense), jnp.sin(dense)))
    cos = cos.reshape(b, s, n_freq)
    sin = sin.reshape(b, s, n_freq)
    lead = (b, s, MLA_NOPE)
    tail = (b, s, LANES - MLA_NOPE - MLA_ROPE)
    cos_t = jnp.concatenate([jnp.ones(lead, F32), cos, cos, jnp.zeros(tail, F32)], axis=-1)
    sin_t = jnp.concatenate([jnp.zeros(lead, F32), -sin, sin, jnp.zeros(tail, F32)], axis=-1)
    return cos_t, sin_t


def kernel(x, p, positions, attn_norm, w_in, b_forget, q_norm, w_uq, kv_norm, w_ukv, conv_w,
           mla_out_norm, conv_out_norm, fox_out_norm, w_out, ffn_norm, w_up, ffn_conv_w,
           ffn_conv_b, w_down, ple_norm, w_ple_gate, w_ple, final_norm):
    depth = w_in.shape[0]
    cos_t, sin_t = _rope_rows(positions)
    w = _prepare_weights(attn_norm, w_in, b_forget, q_norm, w_uq, kv_norm, w_ukv, conv_w,
                         mla_out_norm, conv_out_norm, fox_out_norm, w_out, ffn_norm, w_up,
                         ffn_conv_w, ffn_conv_b, w_down, ple_norm, w_ple_gate, w_ple, final_norm)
    h = x
    for layer in range(depth):
        q, k, vt, oc = _in_call(h, cos_t, sin_t, layer, w)
        o = _attn_call(q, k, vt)
        h = _post_call(o, oc, h, p, layer, w, final=(layer == depth - 1))
    return h
```

```python
import functools
import math

import jax
import jax.numpy as jnp
import numpy as np
from jax import lax
from jax.experimental import pallas as pl
from jax.experimental.pallas import tpu as pltpu

F32 = jnp.float32
BF16 = jnp.bfloat16

D_MODEL = 1024
MLA_HEADS = 6
MLA_NOPE = 64
MLA_ROPE = 32
MLA_V = 64
Q_RANK = 256
KV_RANK = 128
FOX_HEADS = 6
FOX_DIM = 64
FOX_W = FOX_HEADS * FOX_DIM
CONV_WIDTH = 256
CONV_K = 3
D_FF = 2816
PLE_DIM = 256
ROPE_THETA = 10000.0
EPS = 1e-6
LOG2E = math.log2(math.e)

IN_SIZES = (Q_RANK, KV_RANK, MLA_ROPE, CONV_WIDTH, CONV_WIDTH, CONV_WIDTH, FOX_W, FOX_W, FOX_W,
            FOX_HEADS)
O_ZQ, O_ZKV, O_ZR, O_ZB, O_ZC, O_ZH, O_FQ, O_FK, O_FV, O_FF, O_END = (
    sum(IN_SIZES[:n]) for n in range(len(IN_SIZES) + 1))

LANES = 128
HALF = LANES // 2
BF16_SUBLANES = 16
V7X_VMEM_BYTES = 64 * 1024 * 1024
VMEM_LIMIT = V7X_VMEM_BYTES - 8 * 1024 * 1024

HEADS = MLA_HEADS + FOX_HEADS
HW = MLA_HEADS * LANES
ATT_W = HEADS * 64

C_ZQ = 0
C_ZKV = C_ZQ + Q_RANK
C_ZR = C_ZKV + KV_RANK
C_ZRS = C_ZR + LANES
C_ZB = C_ZRS + LANES
C_ZC = C_ZB + CONV_WIDTH
C_ZH = C_ZC + CONV_WIDTH
C_FQ = C_ZH + CONV_WIDTH
C_FK = C_FQ + FOX_W
C_FF = C_FK + FOX_W
N_Z = C_FF + LANES

TS_IN = 512
T_ATT = 512
Q_SPLIT = 2
UNROLL_BELOW = 14
UNROLL_DIAG = 8
LEAD = 1
MASK_BIAS = -1e30
TS_POST = 512
HALO = BF16_SUBLANES
FF_CHUNKS = ((0, 768), (768, 768), (1536, 768), (2304, 512))


def _rms(x, g):
    return x * lax.rsqrt(jnp.mean(x * x, axis=-1, keepdims=True) + EPS) * g


def _split3(x):
    hi = x.astype(BF16)
    r = x - hi.astype(F32)
    mid = r.astype(BF16)
    lo = (r - mid.astype(F32)).astype(BF16)
    return hi, mid, lo


def _dot(a, b):
    return jnp.dot(a, b, preferred_element_type=F32)


def _dot_nt(a, b):
    return lax.dot_general(a, b, (((1,), (1,)), ((), ())), preferred_element_type=F32)


def _layer_spec(layer, *shape):
    return pl.BlockSpec((None,) + shape, lambda *_: (layer,) + (0,) * len(shape))


def _const_spec(*shape):
    return pl.BlockSpec(shape, lambda *_: (0,) * len(shape))


def _in_kernel(x_ref, g_ref, win_ref, qn_ref, wuq_ref, kvn_ref, wuk_ref, wuvt_ref, wfvt_ref,
               cos_ref, sin_ref, cw_ref, cn_ref, bf_ref, place_ref, ones_ref, vones_ref,
               q_ref, k_ref, vt_ref, oc_ref, xs_sc, cum_sc, *, ts):
    @pl.when(pl.program_id(1) == 0)
    def _():
        xs_sc[0:8, :] = jnp.zeros((8, CONV_WIDTH), F32)
        cum_sc[...] = jnp.zeros_like(cum_sc)

    a = _rms(x_ref[0], g_ref[...]).astype(BF16)
    z = _dot(a, win_ref[...])

    def proj(lo, width):
        return z[:, lo:lo + width]

    cos_t = cos_ref[0]
    sin_t = sin_ref[0]
    low_lanes = lax.broadcasted_iota(jnp.int32, (ts, LANES), 1) < HALF
    top_rows = lax.broadcasted_iota(jnp.int32, (LANES, ts), 0) < HALF

    def head_blocks(h):
        return (slice(h * LANES, (h + 1) * LANES),
                slice((h // 2) * LANES, (h // 2 + 1) * LANES), h % 2 == 0)

    def v_row(pair_t, h):
        own, pair, even = head_blocks(h)
        mine = top_rows if even else jnp.logical_not(top_rows)
        ones = jnp.tile(vones_ref[own, :], (1, ts // LANES))
        return jnp.where(mine, pair_t[pair, :], ones).astype(BF16)

    qn = _rms(proj(C_ZQ, Q_RANK), qn_ref[...]).astype(BF16)
    q_main = _dot(qn, wuq_ref[...])
    rope_half = MLA_ROPE // 2
    first_half = lax.broadcasted_iota(jnp.int32, (ts, LANES), 1) < MLA_NOPE + rope_half

    def swap_halves(x):
        return jnp.where(first_half, pltpu.roll(x, LANES - rope_half, axis=1),
                         pltpu.roll(x, rope_half, axis=1))

    mla_scale = LOG2E * (MLA_NOPE + MLA_ROPE) ** -0.5
    kvn = _rms(proj(C_ZKV, KV_RANK), kvn_ref[...]).astype(BF16)
    k_nope = _dot(kvn, wuk_ref[...])
    v_mla_t = _dot_nt(wuvt_ref[...], kvn)
    k_rope = proj(C_ZR, LANES) * cos_t + proj(C_ZRS, LANES) * sin_t
    for h in range(MLA_HEADS):
        own = head_blocks(h)[0]
        q_h = q_main[:, own]
        q_ref[0, h] = ((q_h * cos_t + swap_halves(q_h) * sin_t) * mla_scale).astype(BF16)
        k_ref[0, h] = (k_nope[:, own] + k_rope).astype(BF16)
        vt_ref[0, h, 0] = v_row(v_mla_t, h)

    xc = proj(C_ZC, CONV_WIDTH) * proj(C_ZH, CONV_WIDTH)
    xs_sc[8:8 + ts, :] = xc
    y = (cw_ref[0:1, :] * xs_sc[6:6 + ts, :] + cw_ref[1:2, :] * xs_sc[7:7 + ts, :]
         + cw_ref[2:3, :] * xc)
    oc_ref[0] = _rms(proj(C_ZB, CONV_WIDTH) * y, cn_ref[...]).astype(BF16)
    xs_sc[0:8, :] = xs_sc[ts:ts + 8, :]

    ff = proj(C_FF, LANES) + bf_ref[...]
    log_f = jnp.minimum(ff, 0.0) - jnp.log1p(jnp.exp(-jnp.abs(ff)))
    row = lax.broadcasted_iota(jnp.int32, (ts, ts), 0)
    col = lax.broadcasted_iota(jnp.int32, (ts, ts), 1)
    tri = (col <= row).astype(BF16)
    f_hi, f_mid, f_lo = _split3(log_f)
    cum = _dot(tri, f_hi) + _dot(tri, f_mid) + _dot(tri, f_lo) + cum_sc[0:1, :]
    cum_sc[0:1, :] = cum[ts - 1:ts, :]
    c_hi, c_mid, c_lo = (term.astype(F32) for term in _split3(cum * LOG2E))
    lane = lax.broadcasted_iota(jnp.int32, (ts, LANES), 1)
    nh = FOX_HEADS
    terms = jnp.where(lane < nh, c_hi,
                      jnp.where(lane < 2 * nh, pltpu.roll(c_mid, nh, axis=1),
                                jnp.where(lane < 3 * nh, pltpu.roll(c_lo, 2 * nh, axis=1), 0.0)))
    extras = _dot(terms.astype(BF16), place_ref[...]) + ones_ref[...]
    extra_q = extras[:, :HW]
    extra_k = extras[:, HW:]

    fq = proj(C_FQ, FOX_W) * (LOG2E * FOX_DIM ** -0.5)
    fk = proj(C_FK, FOX_W)
    fv_t = _dot_nt(wfvt_ref[...], a)
    for h in range(FOX_HEADS):
        own, pair, even = head_blocks(h)
        mine = low_lanes if even else jnp.logical_not(low_lanes)
        q_ref[0, MLA_HEADS + h] = jnp.where(mine, fq[:, pair], extra_q[:, own]).astype(BF16)
        k_ref[0, MLA_HEADS + h] = jnp.where(mine, fk[:, pair], extra_k[:, own]).astype(BF16)
        vt_ref[0, MLA_HEADS + h, 0] = v_row(fv_t, h)


def _in_call(h, cos_t, sin_t, layer, w, *, ts=TS_IN):
    b, s, _ = h.shape
    assert ts == T_ATT
    tile = lambda width: pl.BlockSpec((1, ts, width), lambda bi, si: (bi, si, 0))
    head_spec = pl.BlockSpec((1, HEADS, ts, LANES), lambda bi, si: (bi, 0, si, 0))
    head_shape = jax.ShapeDtypeStruct((b, HEADS, s, LANES), BF16)
    vt_spec = pl.BlockSpec((1, HEADS, 1, LANES, ts), lambda bi, si: (bi, 0, si, 0, 0))
    vt_shape = jax.ShapeDtypeStruct((b, HEADS, s // ts, LANES, ts), BF16)
    return pl.pallas_call(
        functools.partial(_in_kernel, ts=ts),
        grid=(b, s // ts),
        in_specs=[
            tile(D_MODEL),
            _layer_spec(layer, 1, D_MODEL),
            _layer_spec(layer, D_MODEL, N_Z),
            _layer_spec(layer, 1, Q_RANK),
            _layer_spec(layer, Q_RANK, HW),
            _layer_spec(layer, 1, KV_RANK),
            _layer_spec(layer, KV_RANK, HW),
            _layer_spec(layer, MLA_HEADS * MLA_V, KV_RANK),
            _layer_spec(layer, FOX_W, D_MODEL),
            tile(LANES),
            tile(LANES),
            _layer_spec(layer, CONV_K, CONV_WIDTH),
            _layer_spec(layer, 1, CONV_WIDTH),
            _layer_spec(layer, 1, LANES),
            _const_spec(LANES, 2 * HW),
            _const_spec(1, 2 * HW),
            _const_spec(HW, LANES),
        ],
        out_specs=[head_spec, head_spec, vt_spec, tile(CONV_WIDTH)],
        out_shape=[head_shape, head_shape, vt_shape,
                   jax.ShapeDtypeStruct((b, s, CONV_WIDTH), BF16)],
        scratch_shapes=[pltpu.VMEM((ts + 8, CONV_WIDTH), F32), pltpu.VMEM((8, LANES), F32)],
        compiler_params=pltpu.CompilerParams(
            dimension_semantics=("arbitrary", "arbitrary"), vmem_limit_bytes=VMEM_LIMIT),
        name="in_proj",
    )(h, w["attn_norm"], w["w_in"], w["q_norm"], w["w_uq"], w["kv_norm"], w["w_uk"], w["w_uv_t"],
      w["w_fv_t"], cos_t, sin_t, w["conv_w"], w["conv_out_norm"], w["b_forget"], w["place"],
      w["extra_ones"], w["v_ones"])


def _attn_kernel(q_ref, k_ref, vt_ref, bias_ref, o_ref, s_sc, m_sc, acc_sc, *, t, nb):
    m_sc[...] = jnp.full_like(m_sc, -jnp.inf)
    acc_sc[...] = jnp.zeros_like(acc_sc)
    row = lax.broadcasted_iota(jnp.int32, (LANES, t), 0)

    tq = t // Q_SPLIT

    def keys_needed(diag, part):
        return (part + 1) * tq if diag else t

    def scores(hh, i, j, slot, diag, part):
        cols = slice(part * tq, (part + 1) * tq)
        nk = keys_needed(diag, part)
        k = k_ref[0, hh, pl.ds(pl.multiple_of(j * t, t), nk), :]
        q = q_ref[0, hh, pl.ds(pl.multiple_of(i * t + part * tq, tq), tq), :]
        s_t = _dot_nt(k, q)
        s_sc[slot, hh, 0:nk, cols] = s_t + bias_ref[0:nk, cols] if diag else s_t

    def consume(hh, i, j, slot, diag, part):
        cols = slice(part * tq, (part + 1) * tq)
        nk = keys_needed(diag, part)
        m_prev = m_sc[hh, i, :, cols]
        m_new = jnp.maximum(m_prev, jnp.max(s_sc[slot, hh, 0:nk, cols], axis=0, keepdims=True))
        alpha = jnp.exp2(m_prev - m_new)
        p_t = jnp.exp2(s_sc[slot, hh, 0:nk, cols] - m_new).astype(BF16)
        acc = alpha * acc_sc[hh, i, :, cols] + _dot(vt_ref[0, hh, j, :, 0:nk], p_t)
        acc_sc[hh, i, :, cols] = acc
        m_sc[hh, i, :, cols] = m_new
        return acc

    def staggered(i, j, ni, nj, slot, diag):
        pieces = [(hh, part) for hh in range(2) for part in range(Q_SPLIT)]
        for hh, part in pieces[:LEAD]:
            scores(hh, ni, nj, 1 - slot, diag, part)
        accs = [[], []]
        for n, (hh, part) in enumerate(pieces):
            if n + LEAD < len(pieces):
                scores(pieces[n + LEAD][0], ni, nj, 1 - slot, diag, pieces[n + LEAD][1])
            accs[hh].append(consume(hh, i, j, slot, diag, part))
        return [jnp.concatenate(parts, axis=1) for parts in accs]

    def below(i, j, slot):
        wrap = j + 1 == i
        ni = jnp.where(wrap, jnp.minimum(i + 1, nb - 1), i)
        nj = jnp.where(wrap, 0, j + 1)
        staggered(i, j, ni, nj, slot, False)
        return ni, nj

    first = (jnp.int32(1), jnp.int32(0))
    for hh in range(2):
        for part in range(Q_SPLIT):
            scores(hh, *first, 0, False, part)
    n_below = nb * (nb - 1) // 2
    assert n_below % UNROLL_BELOW == 0 and nb % UNROLL_DIAG == 0
    assert UNROLL_BELOW % 2 == 0 and UNROLL_DIAG % 2 == 0

    def below_body(_, unit):
        for u in range(UNROLL_BELOW):
            unit = below(*unit, u % 2)
        return unit

    lax.fori_loop(0, n_below // UNROLL_BELOW, below_body, first)

    def diagonal(i, slot):
        ni = jnp.minimum(i + 1, nb - 1)
        acc0, acc1 = staggered(i, i, ni, ni, slot, True)
        o_t = jnp.where(row < HALF, acc0 / acc0[HALF:HALF + 1, :], acc1 / acc1[0:1, :])
        o_ref[0, pl.ds(pl.multiple_of(i * t, t), t), :] = o_t.T
        return ni

    zero = jnp.int32(0)
    for hh in range(2):
        for part in range(Q_SPLIT):
            scores(hh, zero, zero, 0, True, part)

    def diagonal_body(_, i):
        for u in range(UNROLL_DIAG):
            i = diagonal(i, u % 2)
        return i

    lax.fori_loop(0, nb // UNROLL_DIAG, diagonal_body, zero)


def _attn_call(q, k, vt, *, t=T_ATT):
    b, _, s, _ = q.shape
    nb = s // t
    key = lax.broadcasted_iota(jnp.int32, (t, t), 0)
    query = lax.broadcasted_iota(jnp.int32, (t, t), 1)
    bias = jnp.where(key <= query, 0.0, MASK_BIAS).astype(F32)
    seq = pl.BlockSpec((1, 2, s, LANES), lambda bi, g: (bi, g, 0, 0))
    return pl.pallas_call(
        functools.partial(_attn_kernel, t=t, nb=nb),
        grid=(b, HEADS // 2),
        in_specs=[seq, seq,
                  pl.BlockSpec((1, 2, nb, LANES, t), lambda bi, g: (bi, g, 0, 0, 0)),
                  _const_spec(t, t)],
        out_specs=pl.BlockSpec((1, s, LANES), lambda bi, g: (bi, 0, g)),
        out_shape=jax.ShapeDtypeStruct((b, s, ATT_W), F32),
        scratch_shapes=[pltpu.VMEM((2, 2, t, t), F32), pltpu.VMEM((2, nb, 1, t), F32),
                        pltpu.VMEM((2, nb, LANES, t), F32)],
        compiler_params=pltpu.CompilerParams(
            dimension_semantics=("arbitrary", "arbitrary"), vmem_limit_bytes=VMEM_LIMIT),
        name="attention",
    )(q, k, vt, bias)


def _post_kernel(o_ref, op_ref, oc_ref, ocp_ref, h_ref, hp_ref, p_ref, gm_ref, gf_ref, wout_ref,
                 gffn_ref, wup_ref, cw_ref, cb_ref, wd_ref, gple_ref, wpg_ref, wple_ref, fg_ref,
                 out_ref, h1_sc, m_sc, act_sc, *, ts, final):
    half = ATT_W // 2

    def mix(o, oc, h):
        mixed = jnp.concatenate([_rms(o[:, :half], gm_ref[...]).astype(BF16), oc,
                                 _rms(o[:, half:], gf_ref[...]).astype(BF16)], axis=-1)
        return h + _dot(mixed, wout_ref[...])

    keep = (pl.program_id(1) > 0).astype(F32)
    h1_prev = mix(op_ref[0], ocp_ref[0], hp_ref[0]) * keep
    h1 = mix(o_ref[0], oc_ref[0], h_ref[0])
    h1_sc[...] = h1
    m_sc[0:HALO, :] = _rms(h1_prev, gffn_ref[...]).astype(BF16)
    m_sc[HALO:HALO + ts, :] = _rms(h1, gffn_ref[...]).astype(BF16)

    def up_conv(lo, width):
        u = _dot(m_sc[...], wup_ref[:, lo:lo + width])
        return (cw_ref[0:1, lo:lo + width] * u[HALO - 2:HALO - 2 + ts, :]
                + cw_ref[1:2, lo:lo + width] * u[HALO - 1:HALO - 1 + ts, :]
                + cw_ref[2:3, lo:lo + width] * u[HALO:HALO + ts, :] + cb_ref[:, lo:lo + width])

    for lo, width in FF_CHUNKS:
        gate = up_conv(lo, width)
        val = up_conv(D_FF + lo, width)
        act_sc[:, lo:lo + width] = (gate * jax.nn.sigmoid(gate) * val).astype(BF16)
    h2 = h1_sc[...] + _dot(act_sc[...], wd_ref[...])

    gate = jax.nn.sigmoid(_dot(_rms(h2, gple_ref[...]).astype(BF16), wpg_ref[...]))
    out = h2 + gate * _dot(p_ref[0].astype(BF16), wple_ref[...])
    if final:
        out = _rms(out, fg_ref[...])
    out_ref[0] = out


def _post_call(o, oc, h, p, layer, w, *, final, ts=TS_POST):
    b, s, _ = h.shape
    halo_blocks = ts // HALO
    tile = lambda width: pl.BlockSpec((1, ts, width), lambda bi, si: (bi, si, 0))
    halo = lambda width: pl.BlockSpec(
        (1, HALO, width), lambda bi, si: (bi, jnp.maximum(si * halo_blocks - 1, 0), 0))

    def resident(*shape):
        return pl.BlockSpec((None,) + shape, lambda bi, si: (layer,) + (0,) * len(shape),
                            pipeline_mode=pl.Buffered(1))

    return pl.pallas_call(
        functools.partial(_post_kernel, ts=ts, final=final),
        grid=(b, s // ts),
        in_specs=[
            tile(ATT_W), halo(ATT_W), tile(CONV_WIDTH), halo(CONV_WIDTH), tile(D_MODEL),
            halo(D_MODEL),
            pl.BlockSpec((None, 1, ts, PLE_DIM), lambda bi, si: (layer, bi, si, 0)),
            _layer_spec(layer, 1, ATT_W // 2), _layer_spec(layer, 1, ATT_W // 2),
            resident(D_MODEL, D_MODEL),
            _layer_spec(layer, 1, D_MODEL), resident(D_MODEL, 2 * D_FF),
            _layer_spec(layer, CONV_K, 2 * D_FF), _layer_spec(layer, 1, 2 * D_FF),
            resident(D_FF, D_MODEL),
            _layer_spec(layer, 1, D_MODEL), resident(D_MODEL, D_MODEL), resident(PLE_DIM, D_MODEL),
            _const_spec(1, D_MODEL),
        ],
        out_specs=tile(D_MODEL),
        out_shape=jax.ShapeDtypeStruct(h.shape, F32),
        scratch_shapes=[pltpu.VMEM((ts, D_MODEL), F32), pltpu.VMEM((ts + HALO, D_MODEL), BF16),
                        pltpu.VMEM((ts, D_FF), BF16)],
        compiler_params=pltpu.CompilerParams(
            dimension_semantics=("arbitrary", "arbitrary"), vmem_limit_bytes=VMEM_LIMIT),
        name="post_attention",
    )(o, o, oc, oc, h, h, p, w["mla_out_norm"], w["fox_out_norm"], w["w_out"], w["ffn_norm"],
      w["w_up"], w["ffn_conv_w"], w["ffn_conv_b"], w["w_down"], w["ple_norm"], w["w_ple_gate"],
      w["w_ple"], w["final_norm"])


def _extras_base(h):
    return h * LANES + (HALF if h % 2 == 0 else 0)


def _placement():
    m = np.zeros((LANES, 2 * HW), np.float32)
    for t in range(3):
        for h in range(FOX_HEADS):
            m[FOX_HEADS * t + h, _extras_base(h) + t] = 1.0
            m[FOX_HEADS * t + h, HW + _extras_base(h) + 3 + t] = -1.0
    return jnp.asarray(m, BF16)


def _extra_ones():
    row = np.zeros((1, 2 * HW), np.float32)
    for h in range(FOX_HEADS):
        row[0, _extras_base(h) + 3:_extras_base(h) + 6] = 1.0
        row[0, HW + _extras_base(h):HW + _extras_base(h) + 3] = 1.0
    return jnp.asarray(row)


def _v_ones():
    col = np.zeros((HW, LANES), np.float32)
    for h in range(MLA_HEADS):
        col[_extras_base(h)] = 1.0
    return jnp.asarray(col)


def _pad_last(x, before, width):
    return jnp.pad(x, [(0, 0)] * (x.ndim - 1) + [(before, width - before - x.shape[-1])])


def _prepare_weights(attn_norm, w_in, b_forget, q_norm, w_uq, kv_norm, w_ukv, conv_w,
                     mla_out_norm, conv_out_norm, fox_out_norm, w_out, ffn_norm, w_up,
                     ffn_conv_w, ffn_conv_b, w_down, ple_norm, w_ple_gate, w_ple, final_norm):
    depth = w_in.shape[0]
    half = MLA_ROPE // 2
    wr = w_in[..., O_ZR:O_ZB]
    wr_swapped = jnp.concatenate([wr[..., half:], wr[..., :half]], axis=-1)
    w_in_p = jnp.concatenate([
        w_in[..., O_ZQ:O_ZR],
        _pad_last(wr, MLA_NOPE, LANES), _pad_last(wr_swapped, MLA_NOPE, LANES),
        w_in[..., O_ZB:O_FV],
        _pad_last(w_in[..., O_FF:O_END], 0, LANES)], axis=-1).astype(BF16)

    uq = w_uq.reshape(depth, Q_RANK, MLA_HEADS, MLA_NOPE + MLA_ROPE)
    w_uq_p = _pad_last(uq, 0, LANES).reshape(depth, Q_RANK, HW).astype(BF16)
    ukv = w_ukv.reshape(depth, KV_RANK, MLA_HEADS, MLA_NOPE + MLA_V)
    w_uk = _pad_last(ukv[..., :MLA_NOPE], 0, LANES).reshape(depth, KV_RANK, HW).astype(BF16)
    w_uv = ukv[..., MLA_NOPE:].reshape(depth, KV_RANK, MLA_HEADS * MLA_V)

    rows = lambda v: v.reshape(depth, 1, -1)
    return {
        "attn_norm": rows(attn_norm),
        "w_in": w_in_p,
        "q_norm": rows(q_norm),
        "w_uq": w_uq_p,
        "kv_norm": rows(kv_norm),
        "w_uk": w_uk,
        "w_uv_t": jnp.swapaxes(w_uv, 1, 2).astype(BF16),
        "w_fv_t": jnp.swapaxes(w_in[..., O_FV:O_FF], 1, 2).astype(BF16),
        "v_ones": _v_ones(),
        "conv_w": conv_w,
        "conv_out_norm": rows(conv_out_norm),
        "b_forget": rows(_pad_last(b_forget, 0, LANES)),
        "place": _placement(),
        "extra_ones": _extra_ones(),
        "mla_out_norm": rows(mla_out_norm),
        "fox_out_norm": rows(fox_out_norm),
        "w_out": w_out.astype(BF16),
        "ffn_norm": rows(ffn_norm),
        "w_up": w_up.astype(BF16),
        "ffn_conv_w": ffn_conv_w,
        "ffn_conv_b": rows(ffn_conv_b),
        "w_down": w_down.astype(BF16),
        "ple_norm": rows(ple_norm),
        "w_ple_gate": w_ple_gate.astype(BF16),
        "w_ple": w_ple.astype(BF16),
        "final_norm": final_norm.reshape(1, D_MODEL),
    }


def _rope_rows(positions):
    b, s = positions.shape
    n_freq = MLA_ROPE // 2
    inv_freq = ROPE_THETA ** (-jnp.arange(0, MLA_ROPE, 2, dtype=F32) / MLA_ROPE)
    ang = positions.astype(F32)[..., None] * inv_freq
    dense = lax.optimization_barrier(ang.reshape(b, s * n_freq // LANES, LANES))
    cos, sin = lax.optimization_barrier((jnp.cos(dense), jnp.sin(dense)))
    cos = cos.reshape(b, s, n_freq)
    sin = sin.reshape(b, s, n_freq)
    lead = (b, s, MLA_NOPE)
    tail = (b, s, LANES - MLA_NOPE - MLA_ROPE)
    cos_t = jnp.concatenate([jnp.ones(lead, F32), cos, cos, jnp.zeros(tail, F32)], axis=-1)
    sin_t = jnp.concatenate([jnp.zeros(lead, F32), -sin, sin, jnp.zeros(tail, F32)], axis=-1)
    return cos_t, sin_t


def kernel(x, p, positions, attn_norm, w_in, b_forget, q_norm, w_uq, kv_norm, w_ukv, conv_w,
           mla_out_norm, conv_out_norm, fox_out_norm, w_out, ffn_norm, w_up, ffn_conv_w,
           ffn_conv_b, w_down, ple_norm, w_ple_gate, w_ple, final_norm):
    depth = w_in.shape[0]
    cos_t, sin_t = _rope_rows(positions)
    w = _prepare_weights(attn_norm, w_in, b_forget, q_norm, w_uq, kv_norm, w_ukv, conv_w,
                         mla_out_norm, conv_out_norm, fox_out_norm, w_out, ffn_norm, w_up,
                         ffn_conv_w, ffn_conv_b, w_down, ple_norm, w_ple_gate, w_ple, final_norm)
    h = x
    for layer in range(depth):
        q, k, vt, oc = _in_call(h, cos_t, sin_t, layer, w)
        o = _attn_call(q, k, vt)
        h = _post_call(o, oc, h, p, layer, w, final=(layer == depth - 1))
    return h
```

```python
import functools
import math

import jax
import jax.numpy as jnp
import numpy as np
from jax import lax
from jax.experimental import pallas as pl
from jax.experimental.pallas import tpu as pltpu

F32 = jnp.float32
BF16 = jnp.bfloat16

D_MODEL = 1024
MLA_HEADS = 6
MLA_NOPE = 64
MLA_ROPE = 32
MLA_V = 64
Q_RANK = 256
KV_RANK = 128
FOX_HEADS = 6
FOX_DIM = 64
FOX_W = FOX_HEADS * FOX_DIM
CONV_WIDTH = 256
CONV_K = 3
D_FF = 2816
PLE_DIM = 256
ROPE_THETA = 10000.0
EPS = 1e-6
LOG2E = math.log2(math.e)

IN_SIZES = (Q_RANK, KV_RANK, MLA_ROPE, CONV_WIDTH, CONV_WIDTH, CONV_WIDTH, FOX_W, FOX_W, FOX_W,
            FOX_HEADS)
O_ZQ, O_ZKV, O_ZR, O_ZB, O_ZC, O_ZH, O_FQ, O_FK, O_FV, O_FF, O_END = (
    sum(IN_SIZES[:n]) for n in range(len(IN_SIZES) + 1))

LANES = 128
HALF = LANES // 2
BF16_SUBLANES = 16
V7X_VMEM_BYTES = 64 * 1024 * 1024
VMEM_LIMIT = V7X_VMEM_BYTES - 8 * 1024 * 1024

HEADS = MLA_HEADS + FOX_HEADS
HW = MLA_HEADS * LANES
ATT_W = HEADS * 64

C_ZQ = 0
C_ZKV = C_ZQ + Q_RANK
C_ZR = C_ZKV + KV_RANK
C_ZRS = C_ZR + LANES
C_ZB = C_ZRS + LANES
C_ZC = C_ZB + CONV_WIDTH
C_ZH = C_ZC + CONV_WIDTH
C_FQ = C_ZH + CONV_WIDTH
C_FK = C_FQ + FOX_W
C_FF = C_FK + FOX_W
N_Z = C_FF + LANES

TS_IN = 512
T_ATT = 512
Q_SPLIT = 2
UNROLL_BELOW = 14
UNROLL_DIAG = 8
LEAD = 1
MASK_BIAS = -1e30
TS_POST = 512
HALO = BF16_SUBLANES
FF_CHUNKS = ((0, 768), (768, 768), (1536, 768), (2304, 512))


def _rms(x, g):
    return x * lax.rsqrt(jnp.mean(x * x, axis=-1, keepdims=True) + EPS) * g


def _split3(x):
    hi = x.astype(BF16)
    r = x - hi.astype(F32)
    mid = r.astype(BF16)
    lo = (r - mid.astype(F32)).astype(BF16)
    return hi, mid, lo


def _dot(a, b):
    return jnp.dot(a, b, preferred_element_type=F32)


def _dot_nt(a, b):
    return lax.dot_general(a, b, (((1,), (1,)), ((), ())), preferred_element_type=F32)


def _layer_spec(layer, *shape):
    return pl.BlockSpec((None,) + shape, lambda *_: (layer,) + (0,) * len(shape))


def _const_spec(*shape):
    return pl.BlockSpec(shape, lambda *_: (0,) * len(shape))


def _in_kernel(x_ref, g_ref, win_ref, qn_ref, wuq_ref, kvn_ref, wuk_ref, wuvt_ref, wfvt_ref,
               cos_ref, sin_ref, cw_ref, cn_ref, bf_ref, place_ref, ones_ref, vones_ref,
               q_ref, k_ref, vt_ref, oc_ref, xs_sc, cum_sc, *, ts):
    @pl.when(pl.program_id(1) == 0)
    def _():
        xs_sc[0:8, :] = jnp.zeros((8, CONV_WIDTH), F32)
        cum_sc[...] = jnp.zeros_like(cum_sc)

    a = _rms(x_ref[0], g_ref[...]).astype(BF16)
    z = _dot(a, win_ref[...])

    def proj(lo, width):
        return z[:, lo:lo + width]

    cos_t = cos_ref[0]
    sin_t = sin_ref[0]
    low_lanes = lax.broadcasted_iota(jnp.int32, (ts, LANES), 1) < HALF
    top_rows = lax.broadcasted_iota(jnp.int32, (LANES, ts), 0) < HALF

    def head_blocks(h):
        return (slice(h * LANES, (h + 1) * LANES),
                slice((h // 2) * LANES, (h // 2 + 1) * LANES), h % 2 == 0)

    def v_row(pair_t, h):
        own, pair, even = head_blocks(h)
        mine = top_rows if even else jnp.logical_not(top_rows)
        ones = jnp.tile(vones_ref[own, :], (1, ts // LANES))
        return jnp.where(mine, pair_t[pair, :], ones).astype(BF16)

    qn = _rms(proj(C_ZQ, Q_RANK), qn_ref[...]).astype(BF16)
    q_main = _dot(qn, wuq_ref[...])
    rope_half = MLA_ROPE // 2
    first_half = lax.broadcasted_iota(jnp.int32, (ts, LANES), 1) < MLA_NOPE + rope_half

    def swap_halves(x):
        return jnp.where(first_half, pltpu.roll(x, LANES - rope_half, axis=1),
                         pltpu.roll(x, rope_half, axis=1))

    mla_scale = LOG2E * (MLA_NOPE + MLA_ROPE) ** -0.5
    kvn = _rms(proj(C_ZKV, KV_RANK), kvn_ref[...]).astype(BF16)
    k_nope = _dot(kvn, wuk_ref[...])
    v_mla_t = _dot_nt(wuvt_ref[...], kvn)
    k_rope = proj(C_ZR, LANES) * cos_t + proj(C_ZRS, LANES) * sin_t
    for h in range(MLA_HEADS):
        own = head_blocks(h)[0]
        q_h = q_main[:, own]
        q_ref[0, h] = ((q_h * cos_t + swap_halves(q_h) * sin_t) * mla_scale).astype(BF16)
        k_ref[0, h] = (k_nope[:, own] + k_rope).astype(BF16)
        vt_ref[0, h, 0] = v_row(v_mla_t, h)

    xc = proj(C_ZC, CONV_WIDTH) * proj(C_ZH, CONV_WIDTH)
    xs_sc[8:8 + ts, :] = xc
    y = (cw_ref[0:1, :] * xs_sc[6:6 + ts, :] + cw_ref[1:2, :] * xs_sc[7:7 + ts, :]
         + cw_ref[2:3, :] * xc)
    oc_ref[0] = _rms(proj(C_ZB, CONV_WIDTH) * y, cn_ref[...]).astype(BF16)
    xs_sc[0:8, :] = xs_sc[ts:ts + 8, :]

    ff = proj(C_FF, LANES) + bf_ref[...]
    log_f = jnp.minimum(ff, 0.0) - jnp.log1p(jnp.exp(-jnp.abs(ff)))
    row = lax.broadcasted_iota(jnp.int32, (ts, ts), 0)
    col = lax.broadcasted_iota(jnp.int32, (ts, ts), 1)
    tri = (col <= row).astype(BF16)
    f_hi, f_mid, f_lo = _split3(log_f)
    cum = _dot(tri, f_hi) + _dot(tri, f_mid) + _dot(tri, f_lo) + cum_sc[0:1, :]
    cum_sc[0:1, :] = cum[ts - 1:ts, :]
    c_hi, c_mid, c_lo = (term.astype(F32) for term in _split3(cum * LOG2E))
    lane = lax.broadcasted_iota(jnp.int32, (ts, LANES), 1)
    nh = FOX_HEADS
    terms = jnp.where(lane < nh, c_hi,
                      jnp.where(lane < 2 * nh, pltpu.roll(c_mid, nh, axis=1),
                                jnp.where(lane < 3 * nh, pltpu.roll(c_lo, 2 * nh, axis=1), 0.0)))
    extras = _dot(terms.astype(BF16), place_ref[...]) + ones_ref[...]
    extra_q = extras[:, :HW]
    extra_k = extras[:, HW:]

    fq = proj(C_FQ, FOX_W) * (LOG2E * FOX_DIM ** -0.5)
    fk = proj(C_FK, FOX_W)
    fv_t = _dot_nt(wfvt_ref[...], a)
    for h in range(FOX_HEADS):
        own, pair, even = head_blocks(h)
        mine = low_lanes if even else jnp.logical_not(low_lanes)
        q_ref[0, MLA_HEADS + h] = jnp.where(mine, fq[:, pair], extra_q[:, own]).astype(BF16)
        k_ref[0, MLA_HEADS + h] = jnp.where(mine, fk[:, pair], extra_k[:, own]).astype(BF16)
        vt_ref[0, MLA_HEADS + h, 0] = v_row(fv_t, h)


def _in_call(h, cos_t, sin_t, layer, w, *, ts=TS_IN):
    b, s, _ = h.shape
    assert ts == T_ATT
    tile = lambda width: pl.BlockSpec((1, ts, width), lambda bi, si: (bi, si, 0))
    head_spec = pl.BlockSpec((1, HEADS, ts, LANES), lambda bi, si: (bi, 0, si, 0))
    head_shape = jax.ShapeDtypeStruct((b, HEADS, s, LANES), BF16)
    vt_spec = pl.BlockSpec((1, HEADS, 1, LANES, ts), lambda bi, si: (bi, 0, si, 0, 0))
    vt_shape = jax.ShapeDtypeStruct((b, HEADS, s // ts, LANES, ts), BF16)
    return pl.pallas_call(
        functools.partial(_in_kernel, ts=ts),
        grid=(b, s // ts),
        in_specs=[
            tile(D_MODEL),
            _layer_spec(layer, 1, D_MODEL),
            _layer_spec(layer, D_MODEL, N_Z),
            _layer_spec(layer, 1, Q_RANK),
            _layer_spec(layer, Q_RANK, HW),
            _layer_spec(layer, 1, KV_RANK),
            _layer_spec(layer, KV_RANK, HW),
            _layer_spec(layer, MLA_HEADS * MLA_V, KV_RANK),
            _layer_spec(layer, FOX_W, D_MODEL),
            tile(LANES),
            tile(LANES),
            _layer_spec(layer, CONV_K, CONV_WIDTH),
            _layer_spec(layer, 1, CONV_WIDTH),
            _layer_spec(layer, 1, LANES),
            _const_spec(LANES, 2 * HW),
            _const_spec(1, 2 * HW),
            _const_spec(HW, LANES),
        ],
        out_specs=[head_spec, head_spec, vt_spec, tile(CONV_WIDTH)],
        out_shape=[head_shape, head_shape, vt_shape,
                   jax.ShapeDtypeStruct((b, s, CONV_WIDTH), BF16)],
        scratch_shapes=[pltpu.VMEM((ts + 8, CONV_WIDTH), F32), pltpu.VMEM((8, LANES), F32)],
        compiler_params=pltpu.CompilerParams(
            dimension_semantics=("arbitrary", "arbitrary"), vmem_limit_bytes=VMEM_LIMIT),
        name="in_proj",
    )(h, w["attn_norm"], w["w_in"], w["q_norm"], w["w_uq"], w["kv_norm"], w["w_uk"], w["w_uv_t"],
      w["w_fv_t"], cos_t, sin_t, w["conv_w"], w["conv_out_norm"], w["b_forget"], w["place"],
      w["extra_ones"], w["v_ones"])


def _attn_kernel(q_ref, k_ref, vt_ref, bias_ref, o_ref, s_sc, m_sc, acc_sc, *, t, nb):
    m_sc[...] = jnp.full_like(m_sc, -jnp.inf)
    acc_sc[...] = jnp.zeros_like(acc_sc)
    row = lax.broadcasted_iota(jnp.int32, (LANES, t), 0)

    tq = t // Q_SPLIT

    def keys_needed(diag, part):
        return (part + 1) * tq if diag else t

    def scores(hh, i, j, slot, diag, part):
        cols = slice(part * tq, (part + 1) * tq)
        nk = keys_needed(diag, part)
        k = k_ref[0, hh, pl.ds(pl.multiple_of(j * t, t), nk), :]
        q = q_ref[0, hh, pl.ds(pl.multiple_of(i * t + part * tq, tq), tq), :]
        s_t = _dot_nt(k, q)
        s_sc[slot, hh, 0:nk, cols] = s_t + bias_ref[0:nk, cols] if diag else s_t

    def consume(hh, i, j, slot, diag, part):
        cols = slice(part * tq, (part + 1) * tq)
        nk = keys_needed(diag, part)
        m_prev = m_sc[hh, i, :, cols]
        m_new = jnp.maximum(m_prev, jnp.max(s_sc[slot, hh, 0:nk, cols], axis=0, keepdims=True))
        alpha = jnp.exp2(m_prev - m_new)
        p_t = jnp.exp2(s_sc[slot, hh, 0:nk, cols] - m_new).astype(BF16)
        acc = alpha * acc_sc[hh, i, :, cols] + _dot(vt_ref[0, hh, j, :, 0:nk], p_t)
        acc_sc[hh, i, :, cols] = acc
        m_sc[hh, i, :, cols] = m_new
        return acc

    def staggered(i, j, ni, nj, slot, diag):
        pieces = [(hh, part) for hh in range(2) for part in range(Q_SPLIT)]
        for hh, part in pieces[:LEAD]:
            scores(hh, ni, nj, 1 - slot, diag, part)
        accs = [[], []]
        for n, (hh, part) in enumerate(pieces):
            if n + LEAD < len(pieces):
                scores(pieces[n + LEAD][0], ni, nj, 1 - slot, diag, pieces[n + LEAD][1])
            accs[hh].append(consume(hh, i, j, slot, diag, part))
        return [jnp.concatenate(parts, axis=1) for parts in accs]

    def below(i, j, slot):
        wrap = j + 1 == i
        ni = jnp.where(wrap, jnp.minimum(i + 1, nb - 1), i)
        nj = jnp.where(wrap, 0, j + 1)
        staggered(i, j, ni, nj, slot, False)
        return ni, nj

    first = (jnp.int32(1), jnp.int32(0))
    for hh in range(2):
        for part in range(Q_SPLIT):
            scores(hh, *first, 0, False, part)
    n_below = nb * (nb - 1) // 2
    assert n_below % UNROLL_BELOW == 0 and nb % UNROLL_DIAG == 0
    assert UNROLL_BELOW % 2 == 0 and UNROLL_DIAG % 2 == 0

    def below_body(_, unit):
        for u in range(UNROLL_BELOW):
            unit = below(*unit, u % 2)
        return unit

    lax.fori_loop(0, n_below // UNROLL_BELOW, below_body, first)

    def diagonal(i, slot):
        ni = jnp.minimum(i + 1, nb - 1)
        acc0, acc1 = staggered(i, i, ni, ni, slot, True)
        o_t = jnp.where(row < HALF, acc0 / acc0[HALF:HALF + 1, :], acc1 / acc1[0:1, :])
        o_ref[0, pl.ds(pl.multiple_of(i * t, t), t), :] = o_t.T
        return ni

    zero = jnp.int32(0)
    for hh in range(2):
        for part in range(Q_SPLIT):
            scores(hh, zero, zero, 0, True, part)

    def diagonal_body(_, i):
        for u in range(UNROLL_DIAG):
            i = diagonal(i, u % 2)
        return i

    lax.fori_loop(0, nb // UNROLL_DIAG, diagonal_body, zero)


def _attn_call(q, k, vt, *, t=T_ATT):
    b, _, s, _ = q.shape
    nb = s // t
    key = lax.broadcasted_iota(jnp.int32, (t, t), 0)
    query = lax.broadcasted_iota(jnp.int32, (t, t), 1)
    bias = jnp.where(key <= query, 0.0, MASK_BIAS).astype(F32)
    seq = pl.BlockSpec((1, 2, s, LANES), lambda bi, g: (bi, g, 0, 0))
    return pl.pallas_call(
        functools.partial(_attn_kernel, t=t, nb=nb),
        grid=(b, HEADS // 2),
        in_specs=[seq, seq,
                  pl.BlockSpec((1, 2, nb, LANES, t), lambda bi, g: (bi, g, 0, 0, 0)),
                  _const_spec(t, t)],
        out_specs=pl.BlockSpec((1, s, LANES), lambda bi, g: (bi, 0, g)),
        out_shape=jax.ShapeDtypeStruct((b, s, ATT_W), F32),
        scratch_shapes=[pltpu.VMEM((2, 2, t, t), F32), pltpu.VMEM((2, nb, 1, t), F32),
                        pltpu.VMEM((2, nb, LANES, t), F32)],
        compiler_params=pltpu.CompilerParams(
            dimension_semantics=("arbitrary", "arbitrary"), vmem_limit_bytes=VMEM_LIMIT),
        name="attention",
    )(q, k, vt, bias)


def _post_kernel(o_ref, op_ref, oc_ref, ocp_ref, h_ref, hp_ref, p_ref, gm_ref, gf_ref, wout_ref,
                 gffn_ref, wup_ref, cw_ref, cb_ref, wd_ref, gple_ref, wpg_ref, wple_ref, fg_ref,
                 out_ref, h1_sc, m_sc, act_sc, *, ts, final):
    half = ATT_W // 2

    def mix(o, oc, h):
        mixed = jnp.concatenate([_rms(o[:, :half], gm_ref[...]).astype(BF16), oc,
                                 _rms(o[:, half:], gf_ref[...]).astype(BF16)], axis=-1)
        return h + _dot(mixed, wout_ref[...])

    keep = (pl.program_id(1) > 0).astype(F32)
    h1_all = mix(jnp.concatenate([op_ref[0], o_ref[0]], axis=0),
                 jnp.concatenate([ocp_ref[0], oc_ref[0]], axis=0),
                 jnp.concatenate([hp_ref[0], h_ref[0]], axis=0))
    h1_prev = h1_all[0:HALO, :] * keep
    h1 = h1_all[HALO:HALO + ts, :]
    h1_sc[...] = h1
    m_sc[0:HALO, :] = _rms(h1_prev, gffn_ref[...]).astype(BF16)
    m_sc[HALO:HALO + ts, :] = _rms(h1, gffn_ref[...]).astype(BF16)

    def up_conv(lo, width):
        u = _dot(m_sc[...], wup_ref[:, lo:lo + width])
        return (cw_ref[0:1, lo:lo + width] * u[HALO - 2:HALO - 2 + ts, :]
                + cw_ref[1:2, lo:lo + width] * u[HALO - 1:HALO - 1 + ts, :]
                + cw_ref[2:3, lo:lo + width] * u[HALO:HALO + ts, :] + cb_ref[:, lo:lo + width])

    for lo, width in FF_CHUNKS:
        gate = up_conv(lo, width)
        val = up_conv(D_FF + lo, width)
        act_sc[:, lo:lo + width] = (gate * jax.nn.sigmoid(gate) * val).astype(BF16)
    h2 = h1_sc[...] + _dot(act_sc[...], wd_ref[...])

    gate = jax.nn.sigmoid(_dot(_rms(h2, gple_ref[...]).astype(BF16), wpg_ref[...]))
    out = h2 + gate * _dot(p_ref[0].astype(BF16), wple_ref[...])
    if final:
        out = _rms(out, fg_ref[...])
    out_ref[0] = out


def _post_call(o, oc, h, p, layer, w, *, final, ts=TS_POST):
    b, s, _ = h.shape
    halo_blocks = ts // HALO
    tile = lambda width: pl.BlockSpec((1, ts, width), lambda bi, si: (bi, si, 0))
    halo = lambda width: pl.BlockSpec(
        (1, HALO, width), lambda bi, si: (bi, jnp.maximum(si * halo_blocks - 1, 0), 0))

    def resident(*shape):
        return pl.BlockSpec((None,) + shape, lambda bi, si: (layer,) + (0,) * len(shape),
                            pipeline_mode=pl.Buffered(1))

    return pl.pallas_call(
        functools.partial(_post_kernel, ts=ts, final=final),
        grid=(b, s // ts),
        in_specs=[
            tile(ATT_W), halo(ATT_W), tile(CONV_WIDTH), halo(CONV_WIDTH), tile(D_MODEL),
            halo(D_MODEL),
            pl.BlockSpec((None, 1, ts, PLE_DIM), lambda bi, si: (layer, bi, si, 0)),
            _layer_spec(layer, 1, ATT_W // 2), _layer_spec(layer, 1, ATT_W // 2),
            resident(D_MODEL, D_MODEL),
            _layer_spec(layer, 1, D_MODEL), resident(D_MODEL, 2 * D_FF),
            _layer_spec(layer, CONV_K, 2 * D_FF), _layer_spec(layer, 1, 2 * D_FF),
            resident(D_FF, D_MODEL),
            _layer_spec(layer, 1, D_MODEL), resident(D_MODEL, D_MODEL), resident(PLE_DIM, D_MODEL),
            _const_spec(1, D_MODEL),
        ],
        out_specs=tile(D_MODEL),
        out_shape=jax.ShapeDtypeStruct(h.shape, F32),
        scratch_shapes=[pltpu.VMEM((ts, D_MODEL), F32), pltpu.VMEM((ts + HALO, D_MODEL), BF16),
                        pltpu.VMEM((ts, D_FF), BF16)],
        compiler_params=pltpu.CompilerParams(
            dimension_semantics=("arbitrary", "arbitrary"), vmem_limit_bytes=VMEM_LIMIT),
        name="post_attention",
    )(o, o, oc, oc, h, h, p, w["mla_out_norm"], w["fox_out_norm"], w["w_out"], w["ffn_norm"],
      w["w_up"], w["ffn_conv_w"], w["ffn_conv_b"], w["w_down"], w["ple_norm"], w["w_ple_gate"],
      w["w_ple"], w["final_norm"])


def _extras_base(h):
    return h * LANES + (HALF if h % 2 == 0 else 0)


def _placement():
    m = np.zeros((LANES, 2 * HW), np.float32)
    for t in range(3):
        for h in range(FOX_HEADS):
            m[FOX_HEADS * t + h, _extras_base(h) + t] = 1.0
            m[FOX_HEADS * t + h, HW + _extras_base(h) + 3 + t] = -1.0
    return jnp.asarray(m, BF16)


def _extra_ones():
    row = np.zeros((1, 2 * HW), np.float32)
    for h in range(FOX_HEADS):
        row[0, _extras_base(h) + 3:_extras_base(h) + 6] = 1.0
        row[0, HW + _extras_base(h):HW + _extras_base(h) + 3] = 1.0
    return jnp.asarray(row)


def _v_ones():
    col = np.zeros((HW, LANES), np.float32)
    for h in range(MLA_HEADS):
        col[_extras_base(h)] = 1.0
    return jnp.asarray(col)


def _pad_last(x, before, width):
    return jnp.pad(x, [(0, 0)] * (x.ndim - 1) + [(before, width - before - x.shape[-1])])


def _prepare_weights(attn_norm, w_in, b_forget, q_norm, w_uq, kv_norm, w_ukv, conv_w,
                     mla_out_norm, conv_out_norm, fox_out_norm, w_out, ffn_norm, w_up,
                     ffn_conv_w, ffn_conv_b, w_down, ple_norm, w_ple_gate, w_ple, final_norm):
    depth = w_in.shape[0]
    half = MLA_ROPE // 2
    wr = w_in[..., O_ZR:O_ZB]
    wr_swapped = jnp.concatenate([wr[..., half:], wr[..., :half]], axis=-1)
    w_in_p = jnp.concatenate([
        w_in[..., O_ZQ:O_ZR],
        _pad_last(wr, MLA_NOPE, LANES), _pad_last(wr_swapped, MLA_NOPE, LANES),
        w_in[..., O_ZB:O_FV],
        _pad_last(w_in[..., O_FF:O_END], 0, LANES)], axis=-1).astype(BF16)

    uq = w_uq.reshape(depth, Q_RANK, MLA_HEADS, MLA_NOPE + MLA_ROPE)
    w_uq_p = _pad_last(uq, 0, LANES).reshape(depth, Q_RANK, HW).astype(BF16)
    ukv = w_ukv.reshape(depth, KV_RANK, MLA_HEADS, MLA_NOPE + MLA_V)
    w_uk = _pad_last(ukv[..., :MLA_NOPE], 0, LANES).reshape(depth, KV_RANK, HW).astype(BF16)
    w_uv = ukv[..., MLA_NOPE:].reshape(depth, KV_RANK, MLA_HEADS * MLA_V)

    rows = lambda v: v.reshape(depth, 1, -1)
    return {
        "attn_norm": rows(attn_norm),
        "w_in": w_in_p,
        "q_norm": rows(q_norm),
        "w_uq": w_uq_p,
        "kv_norm": rows(kv_norm),
        "w_uk": w_uk,
        "w_uv_t": jnp.swapaxes(w_uv, 1, 2).astype(BF16),
        "w_fv_t": jnp.swapaxes(w_in[..., O_FV:O_FF], 1, 2).astype(BF16),
        "v_ones": _v_ones(),
        "conv_w": conv_w,
        "conv_out_norm": rows(conv_out_norm),
        "b_forget": rows(_pad_last(b_forget, 0, LANES)),
        "place": _placement(),
        "extra_ones": _extra_ones(),
        "mla_out_norm": rows(mla_out_norm),
        "fox_out_norm": rows(fox_out_norm),
        "w_out": w_out.astype(BF16),
        "ffn_norm": rows(ffn_norm),
        "w_up": w_up.astype(BF16),
        "ffn_conv_w": ffn_conv_w,
        "ffn_conv_b": rows(ffn_conv_b),
        "w_down": w_down.astype(BF16),
        "ple_norm": rows(ple_norm),
        "w_ple_gate": w_ple_gate.astype(BF16),
        "w_ple": w_ple.astype(BF16),
        "final_norm": final_norm.reshape(1, D_MODEL),
    }


def _rope_rows(positions):
    b, s = positions.shape
    n_freq = MLA_ROPE // 2
    inv_freq = ROPE_THETA ** (-jnp.arange(0, MLA_ROPE, 2, dtype=F32) / MLA_ROPE)
    ang = positions.astype(F32)[..., None] * inv_freq
    dense = lax.optimization_barrier(ang.reshape(b, s * n_freq // LANES, LANES))
    cos, sin = lax.optimization_barrier((jnp.cos(dense), jnp.sin(dense)))
    cos = cos.reshape(b, s, n_freq)
    sin = sin.reshape(b, s, n_freq)
    lead = (b, s, MLA_NOPE)
    tail = (b, s, LANES - MLA_NOPE - MLA_ROPE)
    cos_t = jnp.concatenate([jnp.ones(lead, F32), cos, cos, jnp.zeros(tail, F32)], axis=-1)
    sin_t = jnp.concatenate([jnp.zeros(lead, F32), -sin, sin, jnp.zeros(tail, F32)], axis=-1)
    return cos_t, sin_t


def kernel(x, p, positions, attn_norm, w_in, b_forget, q_norm, w_uq, kv_norm, w_ukv, conv_w,
           mla_out_norm, conv_out_norm, fox_out_norm, w_out, ffn_norm, w_up, ffn_conv_w,
           ffn_conv_b, w_down, ple_norm, w_ple_gate, w_ple, final_norm):
    depth = w_in.shape[0]
    cos_t, sin_t = _rope_rows(positions)
    w = _prepare_weights(attn_norm, w_in, b_forget, q_norm, w_uq, kv_norm, w_ukv, conv_w,
                         mla_out_norm, conv_out_norm, fox_out_norm, w_out, ffn_norm, w_up,
                         ffn_conv_w, ffn_conv_b, w_down, ple_norm, w_ple_gate, w_ple, final_norm)
    h = x
    for layer in range(depth):
        q, k, vt, oc = _in_call(h, cos_t, sin_t, layer, w)
        o = _attn_call(q, k, vt)
        h = _post_call(o, oc, h, p, layer, w, final=(layer == depth - 1))
    return h
```
